```python
import math
import jax, jax.numpy as jnp
from jax import lax
import numpy as np

D_MODEL = 1024
BATCH = 4
SEQ = 8192
DEPTH = 4

CHUNK = 64
Q_BLOCK = 128
N_MIXERS = 3
N_HEADS = 16
HEAD_DIM = D_MODEL // N_HEADS
INNER = N_HEADS * HEAD_DIM
IDX_HEADS = 8
IDX_DIM = 64
TOPK_MAX = 256
DIFF_HEADS = N_HEADS // 2
DIFF_DIM = HEAD_DIM
ROPE_THETA = 10000.0
LN_EPS = 1e-5
RMS_EPS = 1e-5
ALPHA = (2.0 * DEPTH) ** 0.25
BETA = (8.0 * DEPTH) ** -0.25
A_IN = 4 * INNER + IDX_HEADS * IDX_DIM + IDX_HEADS + IDX_DIM
B_IN = 4 * INNER
C_IN = 4 * INNER

kernel_name = "hybrid_dsa_stickbreak_diffattn_deepnorm"


def _n_layers_of(m):
    return len(range(m, DEPTH, N_MIXERS))


def rope_tables(seq, dim):
    inv = 1.0 / (ROPE_THETA ** (jnp.arange(0, dim, 2, dtype=jnp.float32) / dim))
    ang = jnp.arange(seq, dtype=jnp.float32)[:, None] * inv[None, :]
    return jnp.cos(ang), jnp.sin(ang)


def apply_rope(t, cos, sin):
    c = cos[None, :, None, :].astype(t.dtype)
    s = sin[None, :, None, :].astype(t.dtype)
    t1, t2 = jnp.split(t, 2, axis=-1)
    return jnp.concatenate([t1 * c - t2 * s, t2 * c + t1 * s], axis=-1)


def to_blocks(a):
    b, s = a.shape[0], a.shape[1]
    return jnp.moveaxis(a.reshape(b, s // Q_BLOCK, Q_BLOCK, *a.shape[2:]), 1, 0)


def from_blocks(o):
    o = jnp.moveaxis(o, 0, 1)
    return o.reshape(o.shape[0], o.shape[1] * o.shape[2], *o.shape[3:])


def layer_norm(x, g, b):
    xf = x.astype(jnp.float32)
    mu = jnp.mean(xf, axis=-1, keepdims=True)
    var = jnp.mean(jnp.square(xf - mu), axis=-1, keepdims=True)
    return ((xf - mu) * lax.rsqrt(var + LN_EPS) * g + b).astype(x.dtype)


def rms_norm(x, g):
    xf = x.astype(jnp.float32)
    return (xf * lax.rsqrt(jnp.mean(xf * xf, axis=-1, keepdims=True) + RMS_EPS) * g).astype(x.dtype)


def dsa_mixer(x, w_in, w_out, cos, sin):
    b, s, _ = x.shape
    k_top = min(TOPK_MAX, s // 4)
    cuts = [INNER, 2 * INNER, 3 * INNER, 4 * INNER,
            4 * INNER + IDX_HEADS * IDX_DIM, 4 * INNER + IDX_HEADS * IDX_DIM + IDX_HEADS]
    q, k, v, g, qi, wi, ki = jnp.split(x @ w_in, cuts, axis=-1)
    q = apply_rope(q.reshape(b, s, N_HEADS, HEAD_DIM), cos, sin)
    k = apply_rope(k.reshape(b, s, N_HEADS, HEAD_DIM), cos, sin)
    v = v.reshape(b, s, N_HEADS, HEAD_DIM)
    qi = apply_rope(qi.reshape(b, s, IDX_HEADS, IDX_DIM), cos, sin)
    ki = apply_rope(ki[:, :, None, :], cos, sin)[:, :, 0, :]
    wi = wi.astype(jnp.float32) * IDX_HEADS ** -0.5
    pos = jnp.arange(s)
    key_chunk = pos // CHUNK

    def block(args):
        qb, qib, wib, pb = args
        q_chunk = pb // CHUNK
        rel = jax.nn.relu(jnp.einsum('bqhd,bsd->bqhs', qib, ki).astype(jnp.float32) * IDX_DIM ** -0.5)
        score = jnp.einsum('bqhs,bqh->bqs', rel, wib)
        adm = key_chunk[None, :] <= q_chunk[:, None]
        score = jnp.where(adm[None], score, -jnp.inf)
        _, idx = lax.top_k(score, k_top)
        ks = jax.vmap(lambda kk, ii: kk[ii])(k, idx)
        vs = jax.vmap(lambda vv, ii: vv[ii])(v, idx)
        logits = jnp.einsum('bqhd,bqkhd->bhqk', qb, ks).astype(jnp.float32) * HEAD_DIM ** -0.5
        sel_ok = (idx // CHUNK) <= q_chunk[None, :, None]
        logits = jnp.where(sel_ok[:, None], logits, -jnp.inf)
        p = jax.nn.softmax(logits, axis=-1).astype(vs.dtype)
        return jnp.einsum('bhqk,bqkhd->bqhd', p, vs)

    o = lax.map(block, (to_blocks(q), to_blocks(qi), to_blocks(wi), pos.reshape(-1, Q_BLOCK)))
    o = from_blocks(o).reshape(b, s, INNER)
    return (o * jax.nn.silu(g)) @ w_out


def stick_breaking_mixer(x, w_in, w_out):
    b, s, _ = x.shape
    q, k, v, g = jnp.split(x @ w_in, 4, axis=-1)
    q = q.reshape(b, s, N_HEADS, HEAD_DIM)
    k = k.reshape(b, s, N_HEADS, HEAD_DIM)
    v = v.reshape(b, s, N_HEADS, HEAD_DIM)
    pos = jnp.arange(s)

    def block(args):
        qb, pb = args
        z = jnp.einsum('bqhd,bshd->bhqs', qb, k).astype(jnp.float32) * HEAD_DIM ** -0.5
        before = pos[None, :] < pb[:, None]
        log_one_minus = jnp.where(before, jax.nn.log_sigmoid(-z), 0.0)
        log_stick = lax.cumsum(log_one_minus, axis=3, reverse=True) - log_one_minus
        a = jnp.where(before, jnp.exp(jax.nn.log_sigmoid(z) + log_stick), 0.0).astype(v.dtype)
        return jnp.einsum('bhqs,bshd->bqhd', a, v)

    o = lax.map(block, (to_blocks(q), pos.reshape(-1, Q_BLOCK)))
    o = from_blocks(o).reshape(b, s, INNER)
    return (o * jax.nn.silu(g)) @ w_out


def diff_mixer(x, w_in, w_out, lq1, lk1, lq2, lk2, sub_g, lambda_init, cos, sin):
    b, s, _ = x.shape
    q, k, v, g = jnp.split(x @ w_in, 4, axis=-1)
    q = apply_rope(q.reshape(b, s, 2 * DIFF_HEADS, DIFF_DIM), cos, sin).reshape(b, s, DIFF_HEADS, 2, DIFF_DIM)
    k = apply_rope(k.reshape(b, s, 2 * DIFF_HEADS, DIFF_DIM), cos, sin).reshape(b, s, DIFF_HEADS, 2, DIFF_DIM)
    v = v.reshape(b, s, DIFF_HEADS, 2 * DIFF_DIM)
    lam = (jnp.exp(jnp.sum(lq1.astype(jnp.float32) * lk1.astype(jnp.float32)))
           - jnp.exp(jnp.sum(lq2.astype(jnp.float32) * lk2.astype(jnp.float32))) + lambda_init)
    pos = jnp.arange(s)
    key_chunk = pos // CHUNK

    def block(args):
        qb, pb = args
        logits = jnp.einsum('bqhmd,bshmd->bhmqs', qb, k).astype(jnp.float32) * DIFF_DIM ** -0.5
        adm = key_chunk[None, :] <= (pb // CHUNK)[:, None]
        p = jax.nn.softmax(jnp.where(adm, logits, -jnp.inf), axis=-1)
        wdiff = (p[:, :, 0] - lam * p[:, :, 1]).astype(v.dtype)
        return jnp.einsum('bhqs,bshe->bqhe', wdiff, v)

    o = from_blocks(lax.map(block, (to_blocks(q), pos.reshape(-1, Q_BLOCK))))
    o = rms_norm(o, sub_g) * (1.0 - lambda_init)
    o = o.reshape(b, s, INNER)
    return (o * jax.nn.silu(g)) @ w_out


def setup_inputs(seed: int = 0) -> dict:
    key = jax.random.key(seed)
    ks = jax.random.split(key, 14)
    na, nb_, nc = _n_layers_of(0), _n_layers_of(1), _n_layers_of(2)

    def in_proj(k, n, width):
        col = jnp.ones((width,), jnp.float32).at[2 * INNER:3 * INNER].set(BETA)
        return jax.random.normal(k, (n, D_MODEL, width), jnp.float32) * D_MODEL ** -0.5 * col

    def out_proj(k, n):
        return jax.random.normal(k, (n, INNER, D_MODEL), jnp.float32) * INNER ** -0.5 * BETA

    return {
        "x": jax.random.normal(ks[0], (BATCH, SEQ, D_MODEL), jnp.float32),
        "w_in_a": in_proj(ks[1], na, A_IN),
        "w_out_a": out_proj(ks[2], na),
        "w_in_b": in_proj(ks[3], nb_, B_IN),
        "w_out_b": out_proj(ks[4], nb_),
        "w_in_c": in_proj(ks[5], nc, C_IN),
        "w_out_c": out_proj(ks[6], nc),
        "lambda_q1": 0.1 * jax.random.normal(ks[7], (nc, DIFF_DIM), jnp.float32),
        "lambda_k1": 0.1 * jax.random.normal(ks[8], (nc, DIFF_DIM), jnp.float32),
        "lambda_q2": 0.1 * jax.random.normal(ks[9], (nc, DIFF_DIM), jnp.float32),
        "lambda_k2": 0.1 * jax.random.normal(ks[10], (nc, DIFF_DIM), jnp.float32),
        "subln_g": 1.0 + 0.02 * jax.random.normal(ks[11], (nc, 2 * DIFF_DIM), jnp.float32),
        "ln_g": 1.0 + 0.02 * jax.random.normal(ks[12], (DEPTH, D_MODEL), jnp.float32),
        "ln_b": 0.02 * jax.random.normal(ks[13], (DEPTH, D_MODEL), jnp.float32),
    }


def reference(x, w_in_a, w_out_a, w_in_b, w_out_b, w_in_c, w_out_c,
              lambda_q1, lambda_k1, lambda_q2, lambda_k2, subln_g, ln_g, ln_b):
    s = x.shape[1]
    cos_a, sin_a = rope_tables(s, HEAD_DIM)
    cos_i, sin_i = rope_tables(s, IDX_DIM)
    cos_c, sin_c = rope_tables(s, DIFF_DIM)
    for i in range(DEPTH):
        m, j = i % N_MIXERS, i // N_MIXERS
        if m == 0:
            y = dsa_mixer(x, w_in_a[j], w_out_a[j], cos_a, sin_a) if HEAD_DIM == IDX_DIM else None
        elif m == 1:
            y = stick_breaking_mixer(x, w_in_b[j], w_out_b[j])
        else:
            lambda_init = 0.8 - 0.6 * math.exp(-0.3 * i)
            y = diff_mixer(x, w_in_c[j], w_out_c[j], lambda_q1[j], lambda_k1[j],
                           lambda_q2[j], lambda_k2[j], subln_g[j], lambda_init, cos_c, sin_c)
        x = layer_norm(ALPHA * x + y, ln_g[i], ln_b[i])
    return x
```

```python
import functools
import math

import jax
import jax.numpy as jnp
from jax import lax
from jax.experimental import pallas as pl
from jax.experimental.pallas import tpu as pltpu

F32 = jnp.float32
BF16 = jnp.bfloat16

LANE = 128
V7X_VMEM_LIMIT_BYTES = 56 * 1024 * 1024

D_MODEL = 1024
DEPTH = 4
CHUNK = 64
N_MIXERS = 3
N_HEADS = 16
HEAD_DIM = 64
INNER = N_HEADS * HEAD_DIM
N_PAIRS = INNER // LANE
IDX_HEADS = 8
IDX_DIM = 64
IDX_PAIRS = IDX_HEADS * IDX_DIM // LANE
TOPK_MAX = 256
DIFF_DIM = HEAD_DIM
ROPE_THETA = 10000.0
LN_EPS = 1e-5
RMS_EPS = 1e-5
ALPHA = (2.0 * DEPTH) ** 0.25
LOGIT_SCALE = HEAD_DIM ** -0.5
NEG_INF = float("-inf")
INT32_MIN = -2 ** 31

_NT = (((1,), (1,)), ((), ()))


def _params(n_grid_dims):
    return pltpu.CompilerParams(
        dimension_semantics=("arbitrary",) * n_grid_dims,
        vmem_limit_bytes=V7X_VMEM_LIMIT_BYTES,
    )


def _in_proj_kernel(x_ref, w_ref, cos_ref, sa_ref, sb_ref, out_ref, *f32_out, ops, chunk):
    x = x_ref[...].astype(BF16)
    n_blocks = len(ops)
    for c0 in range(0, n_blocks, chunk):
        nb = min(chunk, n_blocks - c0)
        acc = jnp.dot(x, w_ref[:, c0 * LANE:(c0 + nb) * LANE], preferred_element_type=F32)
        for j in range(nb):
            rope, scale, f32_slot = ops[c0 + j]
            t = acc[:, j * LANE:(j + 1) * LANE]
            if rope:
                t = (t * cos_ref[...] + pltpu.roll(t, LANE - 32, 1) * sa_ref[...]
                     + pltpu.roll(t, 32, 1) * sb_ref[...])
            if scale != 1.0:
                t = t * scale
            if f32_slot is None:
                out_ref[c0 + j] = t.astype(BF16)
            else:
                out_ref[c0 + j] = jnp.zeros_like(t, dtype=BF16)
                f32_out[0][f32_slot] = t


def _in_proj(x2d, w, rope_tabs, ops, seq, tm=512, chunk=4):
    m, d = x2d.shape
    n_blocks = len(ops)
    assert w.shape == (d, n_blocks * LANE) and m % tm == 0 and seq % tm == 0
    n_f32 = sum(1 for o in ops if o[2] is not None)
    s_tiles = seq // tm
    out_shape = [jax.ShapeDtypeStruct((n_blocks, m, LANE), BF16)]
    out_specs = [pl.BlockSpec((n_blocks, tm, LANE), lambda i: (0, i, 0))]
    if n_f32:
        out_shape.append(jax.ShapeDtypeStruct((n_f32, m, LANE), F32))
        out_specs.append(pl.BlockSpec((n_f32, tm, LANE), lambda i: (0, i, 0)))
    tab_spec = pl.BlockSpec((tm, LANE), lambda i: (i % s_tiles, 0))
    res = pl.pallas_call(
        functools.partial(_in_proj_kernel, ops=ops, chunk=chunk),
        grid=(m // tm,),
        in_specs=[
            pl.BlockSpec((tm, d), lambda i: (i, 0)),
            pl.BlockSpec((d, n_blocks * LANE), lambda i: (0, 0)),
            tab_spec, tab_spec, tab_spec,
        ],
        out_specs=out_specs,
        out_shape=out_shape,
        compiler_params=_params(1),
        name="in_proj",
    )(x2d, w, *rope_tabs)
    return res if n_f32 else (res[0], None)


def _out_proj_ln_kernel(o_ref, g_ref, x_ref, w_ref, lng_ref, lnb_ref, out_ref):
    parts = []
    for p in range(N_PAIRS):
        g = g_ref[p].astype(F32)
        gate = g * (1.0 / (1.0 + jnp.exp(-g)))
        parts.append((o_ref[p] * gate).astype(BF16))
    og = jnp.concatenate(parts, axis=1)
    y = jnp.dot(og, w_ref[...], preferred_element_type=F32)
    z = ALPHA * x_ref[...] + y
    mu = jnp.mean(z, axis=-1, keepdims=True)
    zc = z - mu
    var = jnp.mean(zc * zc, axis=-1, keepdims=True)
    out_ref[...] = zc * lax.rsqrt(var + LN_EPS) * lng_ref[...] + lnb_ref[...]


def _out_proj_ln(o, proj, x2d, w_out, ln_g, ln_b, tm=512):
    m, d = x2d.shape
    return pl.pallas_call(
        _out_proj_ln_kernel,
        grid=(m // tm,),
        in_specs=[
            pl.BlockSpec((N_PAIRS, tm, LANE), lambda i: (0, i, 0)),
            pl.BlockSpec((N_PAIRS, tm, LANE), lambda i: (3, i, 0)),
            pl.BlockSpec((tm, d), lambda i: (i, 0)),
            pl.BlockSpec((INNER, d), lambda i: (0, 0)),
            pl.BlockSpec((1, d), lambda i: (0, 0)),
            pl.BlockSpec((1, d), lambda i: (0, 0)),
        ],
        out_specs=pl.BlockSpec((tm, d), lambda i: (i, 0)),
        out_shape=jax.ShapeDtypeStruct((m, d), F32),
        compiler_params=_params(1),
        name="out_proj_ln",
    )(o, proj, x2d, w_out, ln_g.reshape(1, d), ln_b.reshape(1, d))


def _lane_is_first_head():
    return lax.broadcasted_iota(jnp.int32, (1, LANE), 1) < HEAD_DIM


def _split_heads(pair_tile, is_a):
    zero = jnp.zeros_like(pair_tile)
    return jnp.where(is_a, pair_tile, zero), jnp.where(is_a, zero, pair_tile)


def _chunk_causal_mask(t):
    r = lax.broadcasted_iota(jnp.int32, (t, t), 0)
    c = lax.broadcasted_iota(jnp.int32, (t, t), 1)
    return (c // CHUNK) <= (r // CHUNK)


def _attn_specs(n_q_blocks, seq, t, pg, k_section, v_section):
    q_spec = pl.BlockSpec((pg, t, LANE), lambda b, h, i: (h, b * n_q_blocks + i, 0))
    k_spec = pl.BlockSpec((pg, seq, LANE), lambda b, h, i: (k_section * N_PAIRS // pg + h, b, 0))
    v_spec = pl.BlockSpec((pg, seq, LANE), lambda b, h, i: (v_section * N_PAIRS // pg + h, b, 0))
    o_spec = pl.BlockSpec((pg, t, LANE), lambda b, h, i: (h, b * n_q_blocks + i, 0))
    return q_spec, k_spec, v_spec, o_spec


def _sb_kernel(q_ref, k_ref, v_ref, o_ref, acc_ref, carry_ref, *, t, pg):
    i = pl.program_id(2)
    is_a = _lane_is_first_head()
    r = lax.broadcasted_iota(jnp.int32, (t, t), 0)
    c = lax.broadcasted_iota(jnp.int32, (t, t), 1)
    before = c < r
    later_keys = jnp.where(r > c, 1.0, 0.0).astype(BF16)

    def block_step(p, kb, q_heads, diagonal):
        ks = k_ref[p, pl.ds(pl.multiple_of(kb * t, t), t), :]
        vs = v_ref[p, pl.ds(pl.multiple_of(kb * t, t), t), :]
        v_heads = _split_heads(vs, is_a)
        pv = jnp.zeros((t, LANE), F32)
        for h in range(2):
            z = lax.dot_general(q_heads[h], ks, _NT, preferred_element_type=F32)
            soft = jnp.log1p(jnp.exp(-jnp.abs(z)))
            log_sig = jnp.minimum(z, 0.0) - soft
            log_om = jnp.minimum(-z, 0.0) - soft
            if diagonal:
                log_om = jnp.where(before, log_om, 0.0)
            hi = log_om.astype(BF16)
            lo = (log_om - hi.astype(F32)).astype(BF16)
            stick = (jnp.dot(hi, later_keys, preferred_element_type=F32)
                     + jnp.dot(lo, later_keys, preferred_element_type=F32)
                     + carry_ref[h][:, :1])
            a = jnp.exp(log_sig + stick)
            if diagonal:
                a = jnp.where(before, a, 0.0)
            pv = pv + jnp.dot(a.astype(BF16), v_heads[h], preferred_element_type=F32)
            carry_ref[h] = carry_ref[h] + jnp.sum(log_om, axis=-1, keepdims=True)
        acc_ref[...] += pv

    def pair_body(p, _):
        q_heads = _split_heads(q_ref[p], is_a)
        acc_ref[...] = jnp.zeros_like(acc_ref)
        carry_ref[...] = jnp.zeros_like(carry_ref)
        block_step(p, i, q_heads, True)

        def kb_body(j, _):
            block_step(p, i - 1 - j, q_heads, False)
            return 0

        lax.fori_loop(0, i, kb_body, 0)
        o_ref[p] = acc_ref[...]
        return 0

    lax.fori_loop(0, pg, pair_body, 0)


def _sb_attention(proj, batch, seq, t=256, pg=2):
    nq = seq // t
    q_spec, k_spec, v_spec, o_spec = _attn_specs(nq, seq, t, pg, 1, 2)
    return pl.pallas_call(
        functools.partial(_sb_kernel, t=t, pg=pg),
        grid=(batch, N_PAIRS // pg, nq),
        in_specs=[q_spec, k_spec, v_spec],
        out_specs=o_spec,
        out_shape=jax.ShapeDtypeStruct((N_PAIRS, batch * seq, LANE), F32),
        scratch_shapes=[pltpu.VMEM((t, LANE), F32), pltpu.VMEM((2, t, LANE), F32)],
        compiler_params=_params(3),
        name="stick_breaking_attention",
    )(proj, proj, proj)


def _softmax_block(h, s, v_tile, m_ref, l_ref, acc_ref):
    m_prev = m_ref[h]
    m_new = jnp.maximum(m_prev, jnp.max(s, axis=-1, keepdims=True))
    m_safe = jnp.where(m_new == NEG_INF, 0.0, m_new)
    alpha = jnp.exp(m_prev - m_safe)
    p = jnp.exp(s - m_safe[:, :1])
    l_ref[h] = alpha * l_ref[h] + jnp.sum(p, axis=-1, keepdims=True)
    acc_ref[h] = alpha * acc_ref[h] + jnp.dot(p.astype(BF16), v_tile, preferred_element_type=F32)
    m_ref[h] = m_new


def _reset_softmax_state(m_ref, l_ref, acc_ref):
    m_ref[...] = jnp.full(m_ref.shape, NEG_INF, F32)
    l_ref[...] = jnp.zeros_like(l_ref)
    acc_ref[...] = jnp.zeros_like(acc_ref)


def _diff_kernel(lam_ref, subg_ref, q_ref, k_ref, v_ref, o_ref, m_ref, l_ref, acc_ref, *, t, pg, lambda_init):
    i = pl.program_id(2)
    is_a = _lane_is_first_head()
    adm = _chunk_causal_mask(t)
    lp = lam_ref[...]
    lam = (jnp.exp(jnp.sum(lp[0:1] * lp[1:2], axis=-1, keepdims=True))
           - jnp.exp(jnp.sum(lp[2:3] * lp[3:4], axis=-1, keepdims=True)) + lambda_init)

    def block_step(p, kb, q_maps, diagonal):
        ks = k_ref[p, pl.ds(pl.multiple_of(kb * t, t), t), :]
        vs = v_ref[p, pl.ds(pl.multiple_of(kb * t, t), t), :]
        for h in range(2):
            s = lax.dot_general(q_maps[h], ks, _NT, preferred_element_type=F32)
            if diagonal:
                s = jnp.where(adm, s, NEG_INF)
            _softmax_block(h, s, vs, m_ref, l_ref, acc_ref)

    def pair_body(p, _):
        q_maps = _split_heads(q_ref[p], is_a)
        _reset_softmax_state(m_ref, l_ref, acc_ref)

        def kb_body(kb, _):
            block_step(p, kb, q_maps, False)
            return 0

        lax.fori_loop(0, i, kb_body, 0)
        block_step(p, i, q_maps, True)
        d = acc_ref[0] / l_ref[0] - lam * (acc_ref[1] / l_ref[1])
        ms = jnp.mean(d * d, axis=-1, keepdims=True)
        o_ref[p] = d * lax.rsqrt(ms + RMS_EPS) * subg_ref[...] * (1.0 - lambda_init)
        return 0

    lax.fori_loop(0, pg, pair_body, 0)


def _diff_attention(proj, lam_params, sub_g, lambda_init, batch, seq, t=256, pg=2):
    nq = seq // t
    q_spec, k_spec, v_spec, o_spec = _attn_specs(nq, seq, t, pg, 1, 2)
    return pl.pallas_call(
        functools.partial(_diff_kernel, t=t, pg=pg, lambda_init=lambda_init),
        grid=(batch, N_PAIRS // pg, nq),
        in_specs=[
            pl.BlockSpec((4, LANE), lambda b, h, i: (0, 0)),
            pl.BlockSpec((1, LANE), lambda b, h, i: (0, 0)),
            q_spec, k_spec, v_spec,
        ],
        out_specs=o_spec,
        out_shape=jax.ShapeDtypeStruct((N_PAIRS, batch * seq, LANE), F32),
        scratch_shapes=[pltpu.VMEM((2, t, LANE), F32)] * 3,
        compiler_params=_params(3),
        name="differential_attention",
    )(lam_params, sub_g, proj, proj, proj)


def _dsa_attn_kernel(bias_ref, q_ref, k_ref, v_ref, o_ref, m_ref, l_ref, acc_ref, *, t, pg):
    i = pl.program_id(2)
    is_a = _lane_is_first_head()

    def pair_body(p, _):
        q_heads = _split_heads(q_ref[p], is_a)
        _reset_softmax_state(m_ref, l_ref, acc_ref)

        def kb_body(kb, _):
            ks = k_ref[p, pl.ds(pl.multiple_of(kb * t, t), t), :]
            vs = v_ref[p, pl.ds(pl.multiple_of(kb * t, t), t), :]
            v_heads = _split_heads(vs, is_a)
            bias = bias_ref[0, 0, kb].astype(F32)
            for h in range(2):
                s = lax.dot_general(q_heads[h], ks, _NT, preferred_element_type=F32) + bias
                _softmax_block(h, s, v_heads[h], m_ref, l_ref, acc_ref)
            return 0

        lax.fori_loop(0, i + 1, kb_body, 0)
        o_ref[p] = jnp.where(is_a, acc_ref[0] / l_ref[0], acc_ref[1] / l_ref[1])
        return 0

    lax.fori_loop(0, pg, pair_body, 0)


def _dsa_attention(proj, bias, batch, seq, t=256, pg=2):
    nq = seq // t
    q_spec, k_spec, v_spec, o_spec = _attn_specs(nq, seq, t, pg, 1, 2)
    return pl.pallas_call(
        functools.partial(_dsa_attn_kernel, t=t, pg=pg),
        grid=(batch, N_PAIRS // pg, nq),
        in_specs=[
            pl.BlockSpec((1, 1, nq, t, t), lambda b, h, i: (b, i, 0, 0, 0)),
            q_spec, k_spec, v_spec,
        ],
        out_specs=o_spec,
        out_shape=jax.ShapeDtypeStruct((N_PAIRS, batch * seq, LANE), F32),
        scratch_shapes=[pltpu.VMEM((2, t, LANE), F32)] * 3,
        compiler_params=_params(3),
        name="dsa_sparse_attention",
    )(bias, proj, proj, proj)


def _dsa_select_kernel(qi_ref, ki_ref, wi_ref, bias_ref, key_ref, thr_ref, cnt_ref, run_ref, *, t, k_top):
    i = pl.program_id(1)
    nq = bias_ref.shape[2]
    n_groups = t // LANE
    is_a = _lane_is_first_head()
    adm = _chunk_causal_mask(t)
    r = lax.broadcasted_iota(jnp.int32, (t, t), 0)
    c = lax.broadcasted_iota(jnp.int32, (t, t), 1)
    earlier_keys = jnp.where(r < c, 1.0, 0.0).astype(BF16)
    wi = wi_ref[0]

    def score_block(kb, diagonal):
        ks = ki_ref[0, pl.ds(pl.multiple_of(kb * t, t), t), :]
        sc = jnp.zeros((t, t), F32)
        for pr in range(IDX_PAIRS):
            for h, qh in enumerate(_split_heads(qi_ref[pr], is_a)):
                rel = jnp.maximum(lax.dot_general(qh, ks, _NT, preferred_element_type=F32), 0.0)
                hh = 2 * pr + h
                sc = sc + rel * wi[:, hh:hh + 1]
        if diagonal:
            sc = jnp.where(adm, sc, NEG_INF)
        bits = pltpu.bitcast(sc, jnp.int32)
        key = bits ^ ((bits >> 31) & 0x7FFFFFFF)
        key_ref[kb] = jnp.where(sc == 0.0, 0, key)

    def score_body(kb, _):
        score_block(kb, False)
        return 0

    lax.fori_loop(0, i, score_body, 0)
    score_block(i, True)

    def count_where(pred):
        cnt_ref[...] = jnp.zeros_like(cnt_ref)

        def body(kb, _):
            kk = key_ref[kb]
            part = cnt_ref[...]
            for g in range(n_groups):
                part = part + jnp.where(pred(kk[:, g * LANE:(g + 1) * LANE]), 1.0, 0.0)
            cnt_ref[...] = part
            return 0

        lax.fori_loop(0, i + 1, body, 0)
        return jnp.sum(cnt_ref[...], axis=-1, keepdims=True)

    thr_ref[...] = jnp.full(thr_ref.shape, INT32_MIN, jnp.int32)

    def bit_body(it, _):
        cand = thr_ref[...] + jnp.left_shift(jnp.int32(1), 31 - it)
        n_ge = count_where(lambda kk: kk >= cand)
        thr_ref[...] = jnp.where(n_ge >= float(k_top), cand, thr_ref[...])
        return 0

    lax.fori_loop(0, 32, bit_body, 0)
    thr = thr_ref[...]
    need = float(k_top) - count_where(lambda kk: kk > thr)

    run_ref[...] = jnp.zeros_like(run_ref)

    def bias_block(kb, diagonal):
        kk = key_ref[kb]
        parts = []
        eq_parts = []
        for g in range(n_groups):
            sl = slice(g * LANE, (g + 1) * LANE)
            eq_parts.append(jnp.where(kk[:, sl] == thr, 1.0, 0.0))
        eq = jnp.concatenate(eq_parts, axis=1)
        rank = jnp.dot(eq.astype(BF16), earlier_keys, preferred_element_type=F32)
        for g in range(n_groups):
            sl = slice(g * LANE, (g + 1) * LANE)
            tie_ok = (rank[:, sl] + run_ref[...]) < need
            parts.append(jnp.where(kk[:, sl] > thr, 0.0,
                                   jnp.where(kk[:, sl] == thr, jnp.where(tie_ok, 0.0, NEG_INF), NEG_INF)))
        bias = jnp.concatenate(parts, axis=1)
        if diagonal:
            bias = jnp.where(adm, bias, NEG_INF)
        bias_ref[0, 0, kb] = bias.astype(BF16)
        run_ref[...] = run_ref[...] + jnp.sum(eq, axis=-1, keepdims=True)

    def bias_body(kb, _):
        bias_block(kb, False)
        return 0

    lax.fori_loop(0, i, bias_body, 0)
    bias_block(i, True)

    def fill_body(kb, _):
        bias_ref[0, 0, kb] = jnp.full((t, t), NEG_INF, BF16)
        return 0

    lax.fori_loop(i + 1, nq, fill_body, 0)


def _dsa_select(proj, wi, batch, seq, t=256):
    nq = seq // t
    k_top = min(TOPK_MAX, seq // 4)
    qi_block = 4 * N_PAIRS // IDX_PAIRS
    ki_block = 4 * N_PAIRS + IDX_PAIRS
    return pl.pallas_call(
        functools.partial(_dsa_select_kernel, t=t, k_top=k_top),
        grid=(batch, nq),
        in_specs=[
            pl.BlockSpec((IDX_PAIRS, t, LANE), lambda b, i: (qi_block, b * nq + i, 0)),
            pl.BlockSpec((1, seq, LANE), lambda b, i: (ki_block, b, 0)),
            pl.BlockSpec((1, t, LANE), lambda b, i: (0, b * nq + i, 0)),
        ],
        out_specs=pl.BlockSpec((1, 1, nq, t, t), lambda b, i: (b, i, 0, 0, 0)),
        out_shape=jax.ShapeDtypeStruct((batch, nq, nq, t, t), BF16),
        scratch_shapes=[
            pltpu.VMEM((nq, t, t), jnp.int32),
            pltpu.VMEM((t, LANE), jnp.int32),
            pltpu.VMEM((t, LANE), F32),
            pltpu.VMEM((t, LANE), F32),
        ],
        compiler_params=_params(2),
        name="dsa_indexer_select",
    )(proj, proj, wi)


def _rope_tables(seq):
    inv = 1.0 / (ROPE_THETA ** (jnp.arange(0, HEAD_DIM, 2, dtype=F32) / HEAD_DIM))
    ang = jnp.arange(seq, dtype=F32)[:, None] * inv[None, :]
    cos, sin = jnp.cos(ang), jnp.sin(ang)
    zero = jnp.zeros_like(sin)
    n_heads = LANE // HEAD_DIM
    cos_t = jnp.tile(cos, (1, 2 * n_heads))
    sin_first = jnp.tile(jnp.concatenate([-sin, zero], axis=1), (1, n_heads))
    sin_second = jnp.tile(jnp.concatenate([zero, sin], axis=1), (1, n_heads))
    return cos_t, sin_first, sin_second


def _stream_ops(q_rope, k_rope):
    plain = (False, 1.0, None)
    return ((q_rope, LOGIT_SCALE, None),) * N_PAIRS + ((k_rope, 1.0, None),) * N_PAIRS + (plain,) * (2 * N_PAIRS)


def _one_layer(layer, x2d, params, rope_tabs, batch, seq):
    d = x2d.shape[1]
    mixer, j = layer % N_MIXERS, layer // N_MIXERS
    main = 4 * INNER
    if mixer == 0:
        w = params["w_in_a"][j]
        w_idx_q = w[:, main:main + IDX_HEADS * IDX_DIM]
        w_idx_w = w[:, main + IDX_HEADS * IDX_DIM:main + IDX_HEADS * IDX_DIM + IDX_HEADS]
        w_idx_k = w[:, main + IDX_HEADS * IDX_DIM + IDX_HEADS:]
        w_pad = jnp.concatenate(
            [w[:, :main], w_idx_q, w_idx_k, w_idx_k, w_idx_w, jnp.zeros((d, LANE - IDX_HEADS), w.dtype)],
            axis=1).astype(BF16)
        ops = (_stream_ops(True, True) + ((True, IDX_DIM ** -0.5, None),) * IDX_PAIRS
               + ((True, 1.0, None), (False, IDX_HEADS ** -0.5, 0)))
        proj, wi = _in_proj(x2d, w_pad, rope_tabs, ops, seq)
        bias = _dsa_select(proj, wi, batch, seq)
        o = _dsa_attention(proj, bias, batch, seq)
        w_out = params["w_out_a"][j]
    elif mixer == 1:
        proj, _ = _in_proj(x2d, params["w_in_b"][j].astype(BF16), rope_tabs, _stream_ops(False, False), seq)
        o = _sb_attention(proj, batch, seq)
        w_out = params["w_out_b"][j]
    else:
        lambda_init = 0.8 - 0.6 * math.exp(-0.3 * layer)
        proj, _ = _in_proj(x2d, params["w_in_c"][j].astype(BF16), rope_tabs, _stream_ops(True, True), seq)
        lam_rows = jnp.stack([params["lambda_q1"][j], params["lambda_k1"][j],
                              params["lambda_q2"][j], params["lambda_k2"][j]]).astype(F32)
        lam_params = jnp.pad(lam_rows, ((0, 0), (0, LANE - DIFF_DIM)))
        sub_g = params["subln_g"][j].reshape(1, 2 * DIFF_DIM).astype(F32)
        o = _diff_attention(proj, lam_params, sub_g, lambda_init, batch, seq)
        w_out = params["w_out_c"][j]
    return _out_proj_ln(o, proj, x2d, w_out.astype(BF16), params["ln_g"][layer], params["ln_b"][layer])


def kernel(x, w_in_a, w_out_a, w_in_b, w_out_b, w_in_c, w_out_c,
           lambda_q1, lambda_k1, lambda_q2, lambda_k2, subln_g, ln_g, ln_b):
    batch, seq, d = x.shape
    params = dict(w_in_a=w_in_a, w_out_a=w_out_a, w_in_b=w_in_b, w_out_b=w_out_b, w_in_c=w_in_c, w_out_c=w_out_c,
                  lambda_q1=lambda_q1, lambda_k1=lambda_k1, lambda_q2=lambda_q2, lambda_k2=lambda_k2,
                  subln_g=subln_g, ln_g=ln_g, ln_b=ln_b)
    x2d = x.reshape(batch * seq, d)
    rope_tabs = _rope_tables(seq)
    for layer in range(DEPTH):
        x2d = _one_layer(layer, x2d, params, rope_tabs, batch, seq)
    return x2d.reshape(batch, seq, d)
```

```python
import functools
import math

import jax
import jax.numpy as jnp
from jax import lax
from jax.experimental import pallas as pl
from jax.experimental.pallas import tpu as pltpu

F32 = jnp.float32
BF16 = jnp.bfloat16

LANE = 128
V7X_VMEM_LIMIT_BYTES = 56 * 1024 * 1024

D_MODEL = 1024
DEPTH = 4
CHUNK = 64
N_MIXERS = 3
N_HEADS = 16
HEAD_DIM = 64
INNER = N_HEADS * HEAD_DIM
N_PAIRS = INNER // LANE
IDX_HEADS = 8
IDX_DIM = 64
IDX_PAIRS = IDX_HEADS * IDX_DIM // LANE
TOPK_MAX = 256
DIFF_DIM = HEAD_DIM
ROPE_THETA = 10000.0
LN_EPS = 1e-5
RMS_EPS = 1e-5
ALPHA = (2.0 * DEPTH) ** 0.25
LOGIT_SCALE = HEAD_DIM ** -0.5
NEG_INF = float("-inf")
INT32_MIN = -2 ** 31

T_BLOCK = 256
PAIRS_PER_STEP = 4

Q_BLOCK0, K_BLOCK0, G_BLOCK0, QI_BLOCK0 = 0, N_PAIRS, 2 * N_PAIRS, 3 * N_PAIRS
KI_BLOCK = QI_BLOCK0 + IDX_PAIRS
WI_BLOCK = KI_BLOCK + 1

_NT = (((1,), (1,)), ((), ()))


def _params(n_grid_dims):
    return pltpu.CompilerParams(
        dimension_semantics=("arbitrary",) * n_grid_dims,
        vmem_limit_bytes=V7X_VMEM_LIMIT_BYTES,
    )


def _in_proj_kernel(x_ref, w_ref, cos_ref, sa_ref, sb_ref, main_ref, vt_ref, *f32_ref, ops, chunk, t):
    x = x_ref[...].astype(BF16)
    n_blocks = len(ops)
    tm = x.shape[0]
    for c0 in range(0, n_blocks, chunk):
        nb = min(chunk, n_blocks - c0)
        acc = jnp.dot(x, w_ref[:, c0 * LANE:(c0 + nb) * LANE], preferred_element_type=F32)
        for j in range(nb):
            rope, scale, (dest, slot) = ops[c0 + j]
            y = acc[:, j * LANE:(j + 1) * LANE]
            if rope:
                y = (y * cos_ref[...] + pltpu.roll(y, LANE - 32, 1) * sa_ref[...]
                     + pltpu.roll(y, 32, 1) * sb_ref[...])
            if scale != 1.0:
                y = y * scale
            if dest == "main":
                main_ref[slot] = y.astype(BF16)
            elif dest == "vt":
                y_t = y.T
                for kb in range(tm // t):
                    vt_ref[slot, kb] = y_t[:, kb * t:(kb + 1) * t].astype(BF16)
            else:
                f32_ref[0][slot] = y


def _in_proj(x2d, w, rope_tabs, ops, seq, tm=512, chunk=4):
    m, d = x2d.shape
    t = T_BLOCK
    assert w.shape == (d, len(ops) * LANE) and m % tm == 0 and seq % tm == 0 and tm % t == 0
    n_main = sum(1 for o in ops if o[2][0] == "main")
    n_f32 = sum(1 for o in ops if o[2][0] == "f32")
    s_tiles = seq // tm
    out_shape = [jax.ShapeDtypeStruct((n_main, m, LANE), BF16),
                 jax.ShapeDtypeStruct((N_PAIRS, m // t, LANE, t), BF16)]
    out_specs = [pl.BlockSpec((n_main, tm, LANE), lambda i: (0, i, 0)),
                 pl.BlockSpec((N_PAIRS, tm // t, LANE, t), lambda i: (0, i, 0, 0))]
    if n_f32:
        out_shape.append(jax.ShapeDtypeStruct((n_f32, m, LANE), F32))
        out_specs.append(pl.BlockSpec((n_f32, tm, LANE), lambda i: (0, i, 0)))
    tab_spec = pl.BlockSpec((tm, LANE), lambda i: (i % s_tiles, 0))
    res = pl.pallas_call(
        functools.partial(_in_proj_kernel, ops=ops, chunk=chunk, t=t),
        grid=(m // tm,),
        in_specs=[
            pl.BlockSpec((tm, d), lambda i: (i, 0)),
            pl.BlockSpec((d, len(ops) * LANE), lambda i: (0, 0)),
            tab_spec, tab_spec, tab_spec,
        ],
        out_specs=out_specs,
        out_shape=out_shape,
        compiler_params=_params(1),
        name="in_proj",
    )(x2d, w, *rope_tabs)
    return res if n_f32 else (res[0], res[1], None)


def _out_proj_ln_kernel(o_ref, g_ref, x_ref, w_ref, lng_ref, lnb_ref, out_ref):
    parts = []
    for p in range(N_PAIRS):
        g = g_ref[p].astype(F32)
        gate = g * (1.0 / (1.0 + jnp.exp(-g)))
        parts.append((o_ref[p] * gate).astype(BF16))
    og = jnp.concatenate(parts, axis=1)
    y = jnp.dot(og, w_ref[...], preferred_element_type=F32)
    z = ALPHA * x_ref[...] + y
    mu = jnp.mean(z, axis=-1, keepdims=True)
    zc = z - mu
    var = jnp.mean(zc * zc, axis=-1, keepdims=True)
    out_ref[...] = zc * lax.rsqrt(var + LN_EPS) * lng_ref[...] + lnb_ref[...]


def _out_proj_ln(o, proj, x2d, w_out, ln_g, ln_b, tm=512):
    m, d = x2d.shape
    return pl.pallas_call(
        _out_proj_ln_kernel,
        grid=(m // tm,),
        in_specs=[
            pl.BlockSpec((N_PAIRS, tm, LANE), lambda i: (0, i, 0)),
            pl.BlockSpec((N_PAIRS, tm, LANE), lambda i: (G_BLOCK0 // N_PAIRS, i, 0)),
            pl.BlockSpec((tm, d), lambda i: (i, 0)),
            pl.BlockSpec((INNER, d), lambda i: (0, 0)),
            pl.BlockSpec((1, d), lambda i: (0, 0)),
            pl.BlockSpec((1, d), lambda i: (0, 0)),
        ],
        out_specs=pl.BlockSpec((tm, d), lambda i: (i, 0)),
        out_shape=jax.ShapeDtypeStruct((m, d), F32),
        compiler_params=_params(1),
        name="out_proj_ln",
    )(o, proj, x2d, w_out, ln_g.reshape(1, d), ln_b.reshape(1, d))


def _head_query(q_pair, h):
    is_a = lax.broadcasted_iota(jnp.int32, (1, LANE), 1) < HEAD_DIM
    keep = is_a if h == 0 else jnp.logical_not(is_a)
    return jnp.where(keep, q_pair, jnp.zeros_like(q_pair))


def _key_query_iota(t):
    key = lax.broadcasted_iota(jnp.int32, (t, t), 0)
    qry = lax.broadcasted_iota(jnp.int32, (t, t), 1)
    return key, qry


def _attn_specs(n_q_blocks, seq, t, pg):
    q_spec = pl.BlockSpec((pg, t, LANE), lambda b, h, i: (Q_BLOCK0 // pg + h, b * n_q_blocks + i, 0))
    k_spec = pl.BlockSpec((pg, seq, LANE), lambda b, h, i: (K_BLOCK0 // pg + h, b, 0))
    vt_spec = pl.BlockSpec((pg, seq // t, LANE, t), lambda b, h, i: (h, b, 0, 0))
    o_spec = pl.BlockSpec((pg, t, LANE), lambda b, h, i: (h, b * n_q_blocks + i, 0))
    return q_spec, k_spec, vt_spec, o_spec


def _key_block(k_ref, p, kb, t):
    return k_ref[p, pl.ds(pl.multiple_of(kb * t, t), t), :]


def _sb_kernel(q_ref, k_ref, vt_ref, o_ref, acc_ref, carry_ref, *, t, pg):
    i = pl.program_id(2)
    key, qry = _key_query_iota(t)
    before = key < qry
    later_keys = jnp.where(qry > key, 1.0, 0.0).astype(BF16)
    acc_ref[...] = jnp.zeros_like(acc_ref)
    carry_ref[...] = jnp.zeros_like(carry_ref)

    def block_step(kb, diagonal):
        zs = [lax.dot_general(_key_block(k_ref, hh // 2, kb, t), _head_query(q_ref[hh // 2], hh % 2), _NT,
                              preferred_element_type=F32) for hh in range(2 * pg)]
        pre, cum = [], []
        for hh in range(2 * pg):
            z = zs[hh]
            soft = jnp.log1p(jnp.exp(-jnp.abs(z)))
            log_om = jnp.minimum(-z, 0.0) - soft
            if diagonal:
                log_om = jnp.where(before, log_om, 0.0)
            hi = log_om.astype(BF16)
            lo = (log_om - hi.astype(F32)).astype(BF16)
            cum.append(jnp.dot(later_keys, hi, preferred_element_type=F32)
                       + jnp.dot(later_keys, lo, preferred_element_type=F32))
            pre.append(jnp.minimum(z, 0.0) - soft + carry_ref[hh])
            carry_ref[hh] = carry_ref[hh] + jnp.sum(log_om, axis=0, keepdims=True)
        for hh in range(2 * pg):
            a = jnp.exp(pre[hh] + cum[hh])
            if diagonal:
                a = jnp.where(before, a, 0.0)
            h = hh % 2
            acc_ref[hh] += jnp.dot(vt_ref[hh // 2, kb, h * HEAD_DIM:(h + 1) * HEAD_DIM, :], a.astype(BF16),
                                   preferred_element_type=F32)

    block_step(i, True)

    def kb_body(j, _):
        block_step(i - 1 - j, False)
        return 0

    lax.fori_loop(0, i, kb_body, 0)
    for p in range(pg):
        o_ref[p] = jnp.concatenate([acc_ref[2 * p], acc_ref[2 * p + 1]], axis=0).T


def _sb_attention(proj, vt, batch, seq, t=T_BLOCK, pg=PAIRS_PER_STEP):
    nq = seq // t
    q_spec, k_spec, vt_spec, o_spec = _attn_specs(nq, seq, t, pg)
    return pl.pallas_call(
        functools.partial(_sb_kernel, t=t, pg=pg),
        grid=(batch, N_PAIRS // pg, nq),
        in_specs=[q_spec, k_spec, vt_spec],
        out_specs=o_spec,
        out_shape=jax.ShapeDtypeStruct((N_PAIRS, batch * seq, LANE), F32),
        scratch_shapes=[pltpu.VMEM((2 * pg, HEAD_DIM, t), F32), pltpu.VMEM((2 * pg, 1, t), F32)],
        compiler_params=_params(3),
        name="stick_breaking_attention",
    )(proj, proj, vt)


def _softmax_block(hh, s, vt_rows, m_ref, l_ref, acc_ref):
    m_prev = m_ref[hh]
    m_new = jnp.maximum(m_prev, jnp.max(s, axis=0, keepdims=True))
    m_safe = jnp.where(m_new == NEG_INF, 0.0, m_new)
    alpha = jnp.exp(m_prev - m_safe)
    p = jnp.exp(s - m_safe)
    l_ref[hh] = alpha * l_ref[hh] + jnp.sum(p, axis=0, keepdims=True)
    acc_ref[hh] = alpha * acc_ref[hh] + jnp.dot(vt_rows, p.astype(BF16), preferred_element_type=F32)
    m_ref[hh] = m_new


def _reset_softmax_state(m_ref, l_ref, acc_ref):
    m_ref[...] = jnp.full(m_ref.shape, NEG_INF, F32)
    l_ref[...] = jnp.zeros_like(l_ref)
    acc_ref[...] = jnp.zeros_like(acc_ref)


def _diff_kernel(lam_ref, subg_ref, q_ref, k_ref, vt_ref, o_ref, m_ref, l_ref, acc_ref, *, t, pg, lambda_init):
    i = pl.program_id(2)
    key, qry = _key_query_iota(t)
    adm = (key // CHUNK) <= (qry // CHUNK)
    _reset_softmax_state(m_ref, l_ref, acc_ref)

    def block_step(kb, diagonal):
        ss = [lax.dot_general(_key_block(k_ref, hh // 2, kb, t), _head_query(q_ref[hh // 2], hh % 2), _NT,
                              preferred_element_type=F32) for hh in range(2 * pg)]
        for hh in range(2 * pg):
            s = jnp.where(adm, ss[hh], NEG_INF) if diagonal else ss[hh]
            _softmax_block(hh, s, vt_ref[hh // 2, kb], m_ref, l_ref, acc_ref)

    def kb_body(kb, _):
        block_step(kb, False)
        return 0

    lax.fori_loop(0, i, kb_body, 0)
    block_step(i, True)

    lp = lam_ref[...]
    lam = (jnp.exp(jnp.sum(lp[0:1] * lp[1:2], axis=-1, keepdims=True))
           - jnp.exp(jnp.sum(lp[2:3] * lp[3:4], axis=-1, keepdims=True)) + lambda_init)
    for p in range(pg):
        d = acc_ref[2 * p] / l_ref[2 * p] - lam * (acc_ref[2 * p + 1] / l_ref[2 * p + 1])
        ms = jnp.mean(d * d, axis=0, keepdims=True)
        o_ref[p] = (d * lax.rsqrt(ms + RMS_EPS)).T * subg_ref[...] * (1.0 - lambda_init)


def _diff_attention(proj, vt, lam_params, sub_g, lambda_init, batch, seq, t=T_BLOCK, pg=PAIRS_PER_STEP):
    nq = seq // t
    q_spec, k_spec, vt_spec, o_spec = _attn_specs(nq, seq, t, pg)
    return pl.pallas_call(
        functools.partial(_diff_kernel, t=t, pg=pg, lambda_init=lambda_init),
        grid=(batch, N_PAIRS // pg, nq),
        in_specs=[
            pl.BlockSpec((4, LANE), lambda b, h, i: (0, 0)),
            pl.BlockSpec((1, LANE), lambda b, h, i: (0, 0)),
            q_spec, k_spec, vt_spec,
        ],
        out_specs=o_spec,
        out_shape=jax.ShapeDtypeStruct((N_PAIRS, batch * seq, LANE), F32),
        scratch_shapes=[pltpu.VMEM((2 * pg, 1, t), F32), pltpu.VMEM((2 * pg, 1, t), F32),
                        pltpu.VMEM((2 * pg, LANE, t), F32)],
        compiler_params=_params(3),
        name="differential_attention",
    )(lam_params, sub_g, proj, proj, vt)


def _dsa_attn_kernel(bias_ref, q_ref, k_ref, vt_ref, o_ref, m_ref, l_ref, acc_ref, *, t, pg):
    i = pl.program_id(2)
    _reset_softmax_state(m_ref, l_ref, acc_ref)

    def kb_body(kb, _):
        selected = bias_ref[0, 0, kb].astype(F32) == 0.0
        ss = [lax.dot_general(_key_block(k_ref, hh // 2, kb, t), _head_query(q_ref[hh // 2], hh % 2), _NT,
                              preferred_element_type=F32) for hh in range(2 * pg)]
        for hh in range(2 * pg):
            h = hh % 2
            _softmax_block(hh, jnp.where(selected, ss[hh], NEG_INF),
                           vt_ref[hh // 2, kb, h * HEAD_DIM:(h + 1) * HEAD_DIM, :], m_ref, l_ref, acc_ref)
        return 0

    lax.fori_loop(0, i + 1, kb_body, 0)
    for p in range(pg):
        o_ref[p] = jnp.concatenate([acc_ref[2 * p] / l_ref[2 * p],
                                    acc_ref[2 * p + 1] / l_ref[2 * p + 1]], axis=0).T


def _dsa_attention(proj, vt, bias, batch, seq, t=T_BLOCK, pg=PAIRS_PER_STEP):
    nq = seq // t
    q_spec, k_spec, vt_spec, o_spec = _attn_specs(nq, seq, t, pg)
    return pl.pallas_call(
        functools.partial(_dsa_attn_kernel, t=t, pg=pg),
        grid=(batch, N_PAIRS // pg, nq),
        in_specs=[
            pl.BlockSpec((1, 1, nq, t, t), lambda b, h, i: (b, i, 0, 0, 0)),
            q_spec, k_spec, vt_spec,
        ],
        out_specs=o_spec,
        out_shape=jax.ShapeDtypeStruct((N_PAIRS, batch * seq, LANE), F32),
        scratch_shapes=[pltpu.VMEM((2 * pg, 1, t), F32), pltpu.VMEM((2 * pg, 1, t), F32),
                        pltpu.VMEM((2 * pg, HEAD_DIM, t), F32)],
        compiler_params=_params(3),
        name="dsa_sparse_attention",
    )(bias, proj, proj, vt)


def _dsa_select_kernel(qi_ref, ki_ref, wi_ref, bias_ref, key_ref, thr_ref, cnt_ref, run_ref, *, t, k_top):
    i = pl.program_id(1)
    nq = bias_ref.shape[2]
    n_groups = t // LANE
    r = lax.broadcasted_iota(jnp.int32, (t, t), 0)
    c = lax.broadcasted_iota(jnp.int32, (t, t), 1)
    adm = (c // CHUNK) <= (r // CHUNK)
    earlier_keys = jnp.where(r < c, 1.0, 0.0).astype(BF16)
    wi = wi_ref[0]

    def score_block(kb, diagonal):
        ks = ki_ref[0, pl.ds(pl.multiple_of(kb * t, t), t), :]
        sc = jnp.zeros((t, t), F32)
        for pr in range(IDX_PAIRS):
            for h in range(2):
                rel = jnp.maximum(
                    lax.dot_general(_head_query(qi_ref[pr], h), ks, _NT, preferred_element_type=F32), 0.0)
                hh = 2 * pr + h
                sc = sc + rel * wi[:, hh:hh + 1]
        if diagonal:
            sc = jnp.where(adm, sc, NEG_INF)
        bits = pltpu.bitcast(sc, jnp.int32)
        srt = bits ^ ((bits >> 31) & 0x7FFFFFFF)
        key_ref[kb] = jnp.where(sc == 0.0, 0, srt)

    def score_body(kb, _):
        score_block(kb, False)
        return 0

    lax.fori_loop(0, i, score_body, 0)
    score_block(i, True)

    def count_where(pred):
        cnt_ref[...] = jnp.zeros_like(cnt_ref)

        def body(kb, _):
            kk = key_ref[kb]
            part = cnt_ref[...]
            for g in range(n_groups):
                part = part + jnp.where(pred(kk[:, g * LANE:(g + 1) * LANE]), 1.0, 0.0)
            cnt_ref[...] = part
            return 0

        lax.fori_loop(0, i + 1, body, 0)
        return jnp.sum(cnt_ref[...], axis=-1, keepdims=True)

    thr_ref[...] = jnp.full(thr_ref.shape, INT32_MIN, jnp.int32)

    def bit_body(it, _):
        cand = thr_ref[...] + jnp.left_shift(jnp.int32(1), 31 - it)
        n_ge = count_where(lambda kk: kk >= cand)
        thr_ref[...] = jnp.where(n_ge >= float(k_top), cand, thr_ref[...])
        return 0

    lax.fori_loop(0, 32, bit_body, 0)
    thr = thr_ref[...]
    need = float(k_top) - count_where(lambda kk: kk > thr)

    run_ref[...] = jnp.zeros_like(run_ref)

    def bias_block(kb, diagonal):
        kk = key_ref[kb]
        eq = jnp.concatenate(
            [jnp.where(kk[:, g * LANE:(g + 1) * LANE] == thr, 1.0, 0.0) for g in range(n_groups)], axis=1)
        rank = jnp.dot(eq.astype(BF16), earlier_keys, preferred_element_type=F32)
        parts = []
        for g in range(n_groups):
            sl = slice(g * LANE, (g + 1) * LANE)
            tie_ok = (rank[:, sl] + run_ref[...]) < need
            parts.append(jnp.where(kk[:, sl] > thr, 0.0,
                                   jnp.where(kk[:, sl] == thr, jnp.where(tie_ok, 0.0, NEG_INF), NEG_INF)))
        bias = jnp.concatenate(parts, axis=1)
        if diagonal:
            bias = jnp.where(adm, bias, NEG_INF)
        bias_ref[0, 0, kb] = bias.T.astype(BF16)
        run_ref[...] = run_ref[...] + jnp.sum(eq, axis=-1, keepdims=True)

    def bias_body(kb, _):
        bias_block(kb, False)
        return 0

    lax.fori_loop(0, i, bias_body, 0)
    bias_block(i, True)

    def fill_body(kb, _):
        bias_ref[0, 0, kb] = jnp.full((t, t), NEG_INF, BF16)
        return 0

    lax.fori_loop(i + 1, nq, fill_body, 0)


def _dsa_select(proj, wi, batch, seq, t=T_BLOCK):
    nq = seq // t
    k_top = min(TOPK_MAX, seq // 4)
    return pl.pallas_call(
        functools.partial(_dsa_select_kernel, t=t, k_top=k_top),
        grid=(batch, nq),
        in_specs=[
            pl.BlockSpec((IDX_PAIRS, t, LANE), lambda b, i: (QI_BLOCK0 // IDX_PAIRS, b * nq + i, 0)),
            pl.BlockSpec((1, seq, LANE), lambda b, i: (KI_BLOCK, b, 0)),
            pl.BlockSpec((1, t, LANE), lambda b, i: (0, b * nq + i, 0)),
        ],
        out_specs=pl.BlockSpec((1, 1, nq, t, t), lambda b, i: (b, i, 0, 0, 0)),
        out_shape=jax.ShapeDtypeStruct((batch, nq, nq, t, t), BF16),
        scratch_shapes=[
            pltpu.VMEM((nq, t, t), jnp.int32),
            pltpu.VMEM((t, LANE), jnp.int32),
            pltpu.VMEM((t, LANE), F32),
            pltpu.VMEM((t, LANE), F32),
        ],
        compiler_params=_params(2),
        name="dsa_indexer_select",
    )(proj, proj, wi)


def _rope_tables(seq):
    inv = 1.0 / (ROPE_THETA ** (jnp.arange(0, HEAD_DIM, 2, dtype=F32) / HEAD_DIM))
    ang = jnp.arange(seq, dtype=F32)[:, None] * inv[None, :]
    cos, sin = jnp.cos(ang), jnp.sin(ang)
    zero = jnp.zeros_like(sin)
    n_heads = LANE // HEAD_DIM
    cos_t = jnp.tile(cos, (1, 2 * n_heads))
    sin_first = jnp.tile(jnp.concatenate([-sin, zero], axis=1), (1, n_heads))
    sin_second = jnp.tile(jnp.concatenate([zero, sin], axis=1), (1, n_heads))
    return cos_t, sin_first, sin_second


def _stream_ops(q_rope, k_rope):
    q_ops = tuple((q_rope, LOGIT_SCALE, ("main", Q_BLOCK0 + p)) for p in range(N_PAIRS))
    k_ops = tuple((k_rope, 1.0, ("main", K_BLOCK0 + p)) for p in range(N_PAIRS))
    v_ops = tuple((False, 1.0, ("vt", p)) for p in range(N_PAIRS))
    g_ops = tuple((False, 1.0, ("main", G_BLOCK0 + p)) for p in range(N_PAIRS))
    return q_ops + k_ops + v_ops + g_ops


def _one_layer(layer, x2d, params, rope_tabs, batch, seq):
    d = x2d.shape[1]
    mixer, j = layer % N_MIXERS, layer // N_MIXERS
    main = 4 * INNER
    if mixer == 0:
        w = params["w_in_a"][j]
        w_idx_q = w[:, main:main + IDX_HEADS * IDX_DIM]
        w_idx_w = w[:, main + IDX_HEADS * IDX_DIM:main + IDX_HEADS * IDX_DIM + IDX_HEADS]
        w_idx_k = w[:, main + IDX_HEADS * IDX_DIM + IDX_HEADS:]
        w_pad = jnp.concatenate(
            [w[:, :main], w_idx_q, w_idx_k, w_idx_k, w_idx_w, jnp.zeros((d, LANE - IDX_HEADS), w.dtype)],
            axis=1).astype(BF16)
        ops = (_stream_ops(True, True)
               + tuple((True, IDX_DIM ** -0.5, ("main", QI_BLOCK0 + p)) for p in range(IDX_PAIRS))
               + ((True, 1.0, ("main", KI_BLOCK)), (False, IDX_HEADS ** -0.5, ("f32", 0))))
        proj, vt, wi = _in_proj(x2d, w_pad, rope_tabs, ops, seq)
        bias = _dsa_select(proj, wi, batch, seq)
        o = _dsa_attention(proj, vt, bias, batch, seq)
        w_out = params["w_out_a"][j]
    elif mixer == 1:
        proj, vt, _ = _in_proj(x2d, params["w_in_b"][j].astype(BF16), rope_tabs, _stream_ops(False, False), seq)
        o = _sb_attention(proj, vt, batch, seq)
        w_out = params["w_out_b"][j]
    else:
        lambda_init = 0.8 - 0.6 * math.exp(-0.3 * layer)
        proj, vt, _ = _in_proj(x2d, params["w_in_c"][j].astype(BF16), rope_tabs, _stream_ops(True, True), seq)
        lam_rows = jnp.stack([params["lambda_q1"][j], params["lambda_k1"][j],
                              params["lambda_q2"][j], params["lambda_k2"][j]]).astype(F32)
        lam_params = jnp.pad(lam_rows, ((0, 0), (0, LANE - DIFF_DIM)))
        sub_g = params["subln_g"][j].reshape(1, 2 * DIFF_DIM).astype(F32)
        o = _diff_attention(proj, vt, lam_params, sub_g, lambda_init, batch, seq)
        w_out = params["w_out_c"][j]
    return _out_proj_ln(o, proj, x2d, w_out.astype(BF16), params["ln_g"][layer], params["ln_b"][layer])


def kernel(x, w_in_a, w_out_a, w_in_b, w_out_b, w_in_c, w_out_c,
           lambda_q1, lambda_k1, lambda_q2, lambda_k2, subln_g, ln_g, ln_b):
    batch, seq, d = x.shape
    params = dict(w_in_a=w_in_a, w_out_a=w_out_a, w_in_b=w_in_b, w_out_b=w_out_b, w_in_c=w_in_c, w_out_c=w_out_c,
                  lambda_q1=lambda_q1, lambda_k1=lambda_k1, lambda_q2=lambda_q2, lambda_k2=lambda_k2,
                  subln_g=subln_g, ln_g=ln_g, ln_b=ln_b)
    x2d = x.reshape(batch * seq, d)
    rope_tabs = _rope_tables(seq)
    for layer in range(DEPTH):
        x2d = _one_layer(layer, x2d, params, rope_tabs, batch, seq)
    return x2d.reshape(batch, seq, d)
```

```python
import functools
import math

import jax
import jax.numpy as jnp
from jax import lax
from jax.experimental import pallas as pl
from jax.experimental.pallas import tpu as pltpu

F32 = jnp.float32
BF16 = jnp.bfloat16

LANE = 128
V7X_VMEM_LIMIT_BYTES = 56 * 1024 * 1024

D_MODEL = 1024
DEPTH = 4
CHUNK = 64
N_MIXERS = 3
N_HEADS = 16
HEAD_DIM = 64
INNER = N_HEADS * HEAD_DIM
N_PAIRS = INNER // LANE
IDX_HEADS = 8
IDX_DIM = 64
IDX_PAIRS = IDX_HEADS * IDX_DIM // LANE
TOPK_MAX = 256
DIFF_DIM = HEAD_DIM
ROPE_THETA = 10000.0
LN_EPS = 1e-5
RMS_EPS = 1e-5
ALPHA = (2.0 * DEPTH) ** 0.25
LOGIT_SCALE = HEAD_DIM ** -0.5
NEG_INF = float("-inf")
INT16_MIN = -2 ** 15
LOG2E = 1.4426950408889634
EXP_UNDERFLOW = -104.5

DENOM_ROWS = 16
T_BLOCK = 256
PAIRS_PER_STEP = 4

Q_BLOCK0, K_BLOCK0, G_BLOCK0, QI_BLOCK0 = 0, N_PAIRS, 2 * N_PAIRS, 3 * N_PAIRS
KI_BLOCK = QI_BLOCK0 + IDX_PAIRS

_NT = (((1,), (1,)), ((), ()))


def _params(n_grid_dims):
    return pltpu.CompilerParams(
        dimension_semantics=("arbitrary",) * n_grid_dims,
        vmem_limit_bytes=V7X_VMEM_LIMIT_BYTES,
    )


def _in_proj_kernel(x_ref, w_ref, cos_ref, sa_ref, sb_ref, main_ref, vt_ref, *f32_ref, ops, chunk, t):
    x = x_ref[...].astype(BF16)
    n_blocks = len(ops)
    tm = x.shape[0]
    for c0 in range(0, n_blocks, chunk):
        nb = min(chunk, n_blocks - c0)
        acc = jnp.dot(x, w_ref[:, c0 * LANE:(c0 + nb) * LANE], preferred_element_type=F32)
        for j in range(nb):
            rope, scale, (dest, slot) = ops[c0 + j]
            y = acc[:, j * LANE:(j + 1) * LANE]
            if rope:
                y = (y * cos_ref[...] + pltpu.roll(y, LANE - 32, 1) * sa_ref[...]
                     + pltpu.roll(y, 32, 1) * sb_ref[...])
            if scale != 1.0:
                y = y * scale
            if dest == "main":
                main_ref[slot] = y.astype(BF16)
            elif dest == "vt":
                y_t = y.T
                for kb in range(tm // t):
                    vt_ref[slot, kb] = y_t[:, kb * t:(kb + 1) * t].astype(BF16)
            else:
                f32_ref[0][slot] = y


def _in_proj(x2d, w, rope_tabs, ops, seq, tm=512, chunk=4):
    m, d = x2d.shape
    t = T_BLOCK
    assert w.shape == (d, len(ops) * LANE) and m % tm == 0 and seq % tm == 0 and tm % t == 0
    n_main = sum(1 for o in ops if o[2][0] == "main")
    n_f32 = sum(1 for o in ops if o[2][0] == "f32")
    s_tiles = seq // tm
    out_shape = [jax.ShapeDtypeStruct((n_main, m, LANE), BF16),
                 jax.ShapeDtypeStruct((N_PAIRS, m // t, LANE, t), BF16)]
    out_specs = [pl.BlockSpec((n_main, tm, LANE), lambda i: (0, i, 0)),
                 pl.BlockSpec((N_PAIRS, tm // t, LANE, t), lambda i: (0, i, 0, 0))]
    if n_f32:
        out_shape.append(jax.ShapeDtypeStruct((n_f32, m, LANE), F32))
        out_specs.append(pl.BlockSpec((n_f32, tm, LANE), lambda i: (0, i, 0)))
    tab_spec = pl.BlockSpec((tm, LANE), lambda i: (i % s_tiles, 0))
    res = pl.pallas_call(
        functools.partial(_in_proj_kernel, ops=ops, chunk=chunk, t=t),
        grid=(m // tm,),
        in_specs=[
            pl.BlockSpec((tm, d), lambda i: (i, 0)),
            pl.BlockSpec((d, len(ops) * LANE), lambda i: (0, 0)),
            tab_spec, tab_spec, tab_spec,
        ],
        out_specs=out_specs,
        out_shape=out_shape,
        compiler_params=_params(1),
        name="in_proj",
    )(x2d, w, *rope_tabs)
    return res if n_f32 else (res[0], res[1], None)


def _out_proj_ln_kernel(o_ref, g_ref, x_ref, w_ref, lng_ref, lnb_ref, out_ref):
    parts = []
    for p in range(N_PAIRS):
        g = g_ref[p].astype(F32)
        gate = g * (1.0 / (1.0 + jnp.exp(-g)))
        parts.append((o_ref[p] * gate).astype(BF16))
    og = jnp.concatenate(parts, axis=1)
    y = jnp.dot(og, w_ref[...], preferred_element_type=F32)
    z = ALPHA * x_ref[...] + y
    mu = jnp.mean(z, axis=-1, keepdims=True)
    zc = z - mu
    var = jnp.mean(zc * zc, axis=-1, keepdims=True)
    out_ref[...] = zc * lax.rsqrt(var + LN_EPS) * lng_ref[...] + lnb_ref[...]


def _out_proj_ln(o, proj, x2d, w_out, ln_g, ln_b, tm=512):
    m, d = x2d.shape
    return pl.pallas_call(
        _out_proj_ln_kernel,
        grid=(m // tm,),
        in_specs=[
            pl.BlockSpec((N_PAIRS, tm, LANE), lambda i: (0, i, 0)),
            pl.BlockSpec((N_PAIRS, tm, LANE), lambda i: (G_BLOCK0 // N_PAIRS, i, 0)),
            pl.BlockSpec((tm, d), lambda i: (i, 0)),
            pl.BlockSpec((INNER, d), lambda i: (0, 0)),
            pl.BlockSpec((1, d), lambda i: (0, 0)),
            pl.BlockSpec((1, d), lambda i: (0, 0)),
        ],
        out_specs=pl.BlockSpec((tm, d), lambda i: (i, 0)),
        out_shape=jax.ShapeDtypeStruct((m, d), F32),
        compiler_params=_params(1),
        name="out_proj_ln",
    )(o, proj, x2d, w_out, ln_g.reshape(1, d), ln_b.reshape(1, d))


def _head_query(q_pair, h):
    is_a = lax.broadcasted_iota(jnp.int32, (1, LANE), 1) < HEAD_DIM
    keep = is_a if h == 0 else jnp.logical_not(is_a)
    return jnp.where(keep, q_pair, jnp.zeros_like(q_pair))


def _key_query_iota(t):
    key = lax.broadcasted_iota(jnp.int32, (t, t), 0)
    qry = lax.broadcasted_iota(jnp.int32, (t, t), 1)
    return key, qry


def _attn_specs(n_q_blocks, seq, t, pg):
    q_spec = pl.BlockSpec((pg, t, LANE), lambda b, h, i: (Q_BLOCK0 // pg + h, b * n_q_blocks + i, 0))
    k_spec = pl.BlockSpec((pg, seq, LANE), lambda b, h, i: (K_BLOCK0 // pg + h, b, 0))
    vt_spec = pl.BlockSpec((pg, seq // t, LANE, t), lambda b, h, i: (h, b, 0, 0))
    o_spec = pl.BlockSpec((pg, t, LANE), lambda b, h, i: (h, b * n_q_blocks + i, 0))
    return q_spec, k_spec, vt_spec, o_spec


def _key_block(k_ref, p, kb, t):
    return k_ref[p, pl.ds(pl.multiple_of(kb * t, t), t), :]


def _sb_kernel(q_ref, k_ref, vt_ref, o_ref, acc_ref, carry_ref, *, t, pg):
    i = pl.program_id(2)
    key, qry = _key_query_iota(t)
    before = key < qry
    later_keys = jnp.where(qry > key, 1.0, 0.0).astype(BF16)
    acc_ref[...] = jnp.zeros_like(acc_ref)
    carry_ref[...] = jnp.zeros_like(carry_ref)

    def block_step(kb, diagonal):
        zs = [lax.dot_general(_key_block(k_ref, hh // 2, kb, t), _head_query(q_ref[hh // 2], hh % 2), _NT,
                              preferred_element_type=F32) for hh in range(2 * pg)]
        pre, cum = [], []
        for hh in range(2 * pg):
            z = zs[hh]
            soft = jnp.log(1.0 + jnp.exp2(jnp.abs(z) * -LOG2E))
            log_sig = jnp.minimum(z, 0.0) - soft
            log_om = log_sig - z
            if diagonal:
                log_om = jnp.where(before, log_om, 0.0)
            hi = log_om.astype(BF16)
            lo = (log_om - hi.astype(F32)).astype(BF16)
            c = (jnp.dot(later_keys, hi, preferred_element_type=F32)
                 + jnp.dot(later_keys, lo, preferred_element_type=F32))
            cum.append(c)
            carry = carry_ref[hh:hh + 1, :]
            pre.append(log_sig + carry)
            carry_ref[hh:hh + 1, :] = carry + c[0:1, :] + log_om[0:1, :]
        for hh in range(2 * pg):
            a = jnp.exp(pre[hh] + cum[hh])
            if diagonal:
                a = jnp.where(before, a, 0.0)
            h = hh % 2
            acc_ref[hh] += jnp.dot(vt_ref[hh // 2, kb, h * HEAD_DIM:(h + 1) * HEAD_DIM, :], a.astype(BF16),
                                   preferred_element_type=F32)

    def some_weight_left():
        return (jnp.max(carry_ref[...]) >= EXP_UNDERFLOW).astype(jnp.int32)

    block_step(i, True)

    def kb_cond(state):
        j, go = state
        return jnp.logical_and(j < i, go == 1)

    def kb_body(state):
        j, _ = state
        block_step(i - 1 - j, False)
        return j + 1, some_weight_left()

    lax.while_loop(kb_cond, kb_body, (jnp.int32(0), some_weight_left()))
    for p in range(pg):
        o_ref[p] = jnp.concatenate([acc_ref[2 * p], acc_ref[2 * p + 1]], axis=0).T


def _sb_attention(proj, vt, batch, seq, t=T_BLOCK, pg=PAIRS_PER_STEP):
    nq = seq // t
    q_spec, k_spec, vt_spec, o_spec = _attn_specs(nq, seq, t, pg)
    return pl.pallas_call(
        functools.partial(_sb_kernel, t=t, pg=pg),
        grid=(batch, N_PAIRS // pg, nq),
        in_specs=[q_spec, k_spec, vt_spec],
        out_specs=o_spec,
        out_shape=jax.ShapeDtypeStruct((N_PAIRS, batch * seq, LANE), F32),
        scratch_shapes=[pltpu.VMEM((2 * pg, HEAD_DIM, t), F32), pltpu.VMEM((2 * pg, t), F32)],
        compiler_params=_params(3),
        name="stick_breaking_attention",
    )(proj, proj, vt)


def _softmax_block(hh, s, vt_rows, m_ref, acc_ref):
    m_prev = m_ref[hh]
    m_new = jnp.maximum(m_prev, jnp.max(s, axis=0, keepdims=True))
    m_safe = jnp.where(m_new == NEG_INF, 0.0, m_new)
    alpha = jnp.exp2(m_prev - m_safe)
    p = jnp.exp2(s - m_safe).astype(BF16)
    lhs = jnp.concatenate([vt_rows, jnp.ones((DENOM_ROWS, vt_rows.shape[1]), BF16)], axis=0)
    acc_ref[hh] = alpha * acc_ref[hh] + jnp.dot(lhs, p, preferred_element_type=F32)
    m_ref[hh] = m_new


def _reset_softmax_state(m_ref, acc_ref):
    m_ref[...] = jnp.full(m_ref.shape, NEG_INF, F32)
    acc_ref[...] = jnp.zeros_like(acc_ref)


def _normalized(acc_ref, hh, n_rows):
    acc = acc_ref[hh]
    return acc[:n_rows] / acc[n_rows:n_rows + 1]


def _score_tiles_into(s_ref, q_ref, k_ref, t, pg, bias_of, slot, kb):
    bias = None if bias_of is None else bias_of(kb)
    for hh in range(2 * pg):
        s = lax.dot_general(_key_block(k_ref, hh // 2, kb, t), _head_query(q_ref[hh // 2], hh % 2), _NT,
                            preferred_element_type=F32)
        s_ref[slot, hh] = s if bias is None else s + bias


def _pipelined_key_loop(i, scores_into, consume):
    scores_into(0, 0)

    def pair_body(jj, _):
        kb = 2 * jj
        scores_into(1, kb + 1)
        consume(0, kb, False)
        scores_into(0, kb + 2)
        consume(1, kb + 1, False)
        return 0

    lax.fori_loop(0, i // 2, pair_body, 0)

    @pl.when(i % 2 == 1)
    def _():
        scores_into(1, i)
        consume(0, i - 1, False)

    consume(i % 2, i, True)


def _diff_kernel(lam_ref, subg_ref, q_ref, k_ref, vt_ref, o_ref, m_ref, acc_ref, s_ref, *, t, pg, lambda_init):
    i = pl.program_id(2)
    key, qry = _key_query_iota(t)
    adm = (key // CHUNK) <= (qry // CHUNK)
    _reset_softmax_state(m_ref, acc_ref)

    def consume(slot, kb, diagonal):
        for hh in range(2 * pg):
            s = s_ref[slot, hh]
            if diagonal:
                s = jnp.where(adm, s, NEG_INF)
            _softmax_block(hh, s, vt_ref[hh // 2, kb], m_ref, acc_ref)

    _pipelined_key_loop(i, functools.partial(_score_tiles_into, s_ref, q_ref, k_ref, t, pg, None), consume)

    lp = lam_ref[...]
    lam = (jnp.exp(jnp.sum(lp[0:1] * lp[1:2], axis=-1, keepdims=True))
           - jnp.exp(jnp.sum(lp[2:3] * lp[3:4], axis=-1, keepdims=True)) + lambda_init)
    for p in range(pg):
        d = _normalized(acc_ref, 2 * p, LANE) - lam * _normalized(acc_ref, 2 * p + 1, LANE)
        ms = jnp.mean(d * d, axis=0, keepdims=True)
        o_ref[p] = (d * lax.rsqrt(ms + RMS_EPS)).T * subg_ref[...] * (1.0 - lambda_init)


def _diff_attention(proj, vt, lam_params, sub_g, lambda_init, batch, seq, t=T_BLOCK, pg=PAIRS_PER_STEP):
    nq = seq // t
    q_spec, k_spec, vt_spec, o_spec = _attn_specs(nq, seq, t, pg)
    return pl.pallas_call(
        functools.partial(_diff_kernel, t=t, pg=pg, lambda_init=lambda_init),
        grid=(batch, N_PAIRS // pg, nq),
        in_specs=[
            pl.BlockSpec((4, LANE), lambda b, h, i: (0, 0)),
            pl.BlockSpec((1, LANE), lambda b, h, i: (0, 0)),
            q_spec, k_spec, vt_spec,
        ],
        out_specs=o_spec,
        out_shape=jax.ShapeDtypeStruct((N_PAIRS, batch * seq, LANE), F32),
        scratch_shapes=[pltpu.VMEM((2 * pg, 1, t), F32), pltpu.VMEM((2 * pg, LANE + DENOM_ROWS, t), F32),
                        pltpu.VMEM((2, 2 * pg, t, t), F32)],
        compiler_params=_params(3),
        name="differential_attention",
    )(lam_params, sub_g, proj, proj, vt)


def _dsa_attn_kernel(bias_ref, q_ref, k_ref, vt_ref, o_ref, m_ref, acc_ref, s_ref, *, t, pg):
    i = pl.program_id(2)
    _reset_softmax_state(m_ref, acc_ref)

    def consume(slot, kb, diagonal):
        del diagonal
        for hh in range(2 * pg):
            h = hh % 2
            _softmax_block(hh, s_ref[slot, hh], vt_ref[hh // 2, kb, h * HEAD_DIM:(h + 1) * HEAD_DIM, :],
                           m_ref, acc_ref)

    def bias_of(kb):
        return bias_ref[0, 0, kb].astype(F32)

    _pipelined_key_loop(i, functools.partial(_score_tiles_into, s_ref, q_ref, k_ref, t, pg, bias_of), consume)
    for p in range(pg):
        o_ref[p] = jnp.concatenate([_normalized(acc_ref, 2 * p, HEAD_DIM),
                                    _normalized(acc_ref, 2 * p + 1, HEAD_DIM)], axis=0).T


def _dsa_attention(proj, vt, bias, batch, seq, t=T_BLOCK, pg=PAIRS_PER_STEP):
    nq = seq // t
    q_spec, k_spec, vt_spec, o_spec = _attn_specs(nq, seq, t, pg)
    return pl.pallas_call(
        functools.partial(_dsa_attn_kernel, t=t, pg=pg),
        grid=(batch, N_PAIRS // pg, nq),
        in_specs=[
            pl.BlockSpec((1, 1, nq, t, t), lambda b, h, i: (b, i, 0, 0, 0)),
            q_spec, k_spec, vt_spec,
        ],
        out_specs=o_spec,
        out_shape=jax.ShapeDtypeStruct((N_PAIRS, batch * seq, LANE), F32),
        scratch_shapes=[pltpu.VMEM((2 * pg, 1, t), F32), pltpu.VMEM((2 * pg, HEAD_DIM + DENOM_ROWS, t), F32),
                        pltpu.VMEM((2, 2 * pg, t, t), F32)],
        compiler_params=_params(3),
        name="dsa_sparse_attention",
    )(bias, proj, proj, vt)


def _dsa_select_kernel(qi_ref, ki_ref, wi_ref, bias_ref, key_ref, hi_ref, lo_ref, thr_hi_ref, thr_lo_ref, cnt_ref,
                       run_ref, *, t, k_top):
    i = pl.program_id(1)
    nq = bias_ref.shape[2]
    n_groups = t // LANE
    r = lax.broadcasted_iota(jnp.int32, (t, t), 0)
    c = lax.broadcasted_iota(jnp.int32, (t, t), 1)
    adm = (c // CHUNK) <= (r // CHUNK)
    earlier_keys = jnp.where(r < c, 1.0, 0.0).astype(BF16)
    wi = wi_ref[0]

    def score_block(kb, diagonal):
        ks = ki_ref[0, pl.ds(pl.multiple_of(kb * t, t), t), :]
        sc = jnp.zeros((t, t), F32)
        for pr in range(IDX_PAIRS):
            for h in range(2):
                rel = jnp.maximum(
                    lax.dot_general(_head_query(qi_ref[pr], h), ks, _NT, preferred_element_type=F32), 0.0)
                hh = 2 * pr + h
                sc = sc + rel * wi[:, hh:hh + 1]
        if diagonal:
            sc = jnp.where(adm, sc, NEG_INF)
        bits = pltpu.bitcast(sc, jnp.int32)
        srt = bits ^ ((bits >> 31) & 0x7FFFFFFF)
        srt = jnp.where(sc == 0.0, 0, srt)
        key_ref[kb] = srt
        hi_ref[kb] = (srt >> 16).astype(jnp.int16)
        lo_ref[kb] = ((srt & 0xFFFF) + INT16_MIN).astype(jnp.int16)

    def score_body(kb, _):
        score_block(kb, False)
        return 0

    lax.fori_loop(0, i, score_body, 0)
    score_block(i, True)

    def count16(ref, pred):
        cnt_ref[...] = jnp.zeros_like(cnt_ref)

        def body(kb, _):
            vals = ref[kb]
            part = cnt_ref[...]
            for g in range(n_groups):
                part = part + jnp.where(pred(vals[:, g * LANE:(g + 1) * LANE]), jnp.int16(1), jnp.int16(0))
            cnt_ref[...] = part
            return 0

        lax.fori_loop(0, i + 1, body, 0)
        return jnp.sum(cnt_ref[...].astype(F32), axis=-1, keepdims=True)

    def bisect16(ref, thr_ref, base):
        thr_ref[...] = jnp.full(thr_ref.shape, INT16_MIN, jnp.int32)

        def bit_body(it, _):
            cand = thr_ref[...] + jnp.left_shift(jnp.int32(1), 15 - it)
            cand16 = cand.astype(jnp.int16)
            n_ge = base + count16(ref, lambda v: v >= cand16)
            thr_ref[...] = jnp.where(n_ge >= float(k_top), cand, thr_ref[...])
            return 0

        lax.fori_loop(0, 16, bit_body, 0)

    bisect16(hi_ref, thr_hi_ref, 0.0)
    thr_hi16 = thr_hi_ref[...].astype(jnp.int16)
    n_above = count16(hi_ref, lambda v: v > thr_hi16)

    def mask_low_body(kb, _):
        lo_ref[kb] = jnp.concatenate(
            [jnp.where(hi_ref[kb, :, g * LANE:(g + 1) * LANE] == thr_hi16,
                       lo_ref[kb, :, g * LANE:(g + 1) * LANE], jnp.int16(INT16_MIN)) for g in range(n_groups)], axis=1)
        return 0

    lax.fori_loop(0, i + 1, mask_low_body, 0)
    bisect16(lo_ref, thr_lo_ref, n_above)
    thr = jnp.left_shift(thr_hi_ref[...], 16) | (thr_lo_ref[...] - INT16_MIN)

    def gt_body(kb, _):
        kk = key_ref[kb]
        part = run_ref[...]
        for g in range(n_groups):
            part = part + jnp.where(kk[:, g * LANE:(g + 1) * LANE] > thr, 1.0, 0.0)
        run_ref[...] = part
        return 0

    run_ref[...] = jnp.zeros_like(run_ref)
    lax.fori_loop(0, i + 1, gt_body, 0)
    need = float(k_top) - jnp.sum(run_ref[...], axis=-1, keepdims=True)
    run_ref[...] = jnp.zeros_like(run_ref)

    def bias_block(kb, diagonal):
        kk = key_ref[kb]
        eq = jnp.concatenate(
            [jnp.where(kk[:, g * LANE:(g + 1) * LANE] == thr, 1.0, 0.0) for g in range(n_groups)], axis=1)
        rank = jnp.dot(eq.astype(BF16), earlier_keys, preferred_element_type=F32)
        parts = []
        for g in range(n_groups):
            sl = slice(g * LANE, (g + 1) * LANE)
            tie_ok = (rank[:, sl] + run_ref[...]) < need
            parts.append(jnp.where(kk[:, sl] > thr, 0.0,
                                   jnp.where(kk[:, sl] == thr, jnp.where(tie_ok, 0.0, NEG_INF), NEG_INF)))
        bias = jnp.concatenate(parts, axis=1)
        if diagonal:
            bias = jnp.where(adm, bias, NEG_INF)
        bias_ref[0, 0, kb] = bias.T.astype(BF16)
        run_ref[...] = run_ref[...] + jnp.sum(eq, axis=-1, keepdims=True)

    def bias_body(kb, _):
        bias_block(kb, False)
        return 0

    lax.fori_loop(0, i, bias_body, 0)
    bias_block(i, True)

    def fill_body(kb, _):
        bias_ref[0, 0, kb] = jnp.full((t, t), NEG_INF, BF16)
        return 0

    lax.fori_loop(i + 1, nq, fill_body, 0)


def _dsa_select(proj, wi, batch, seq, t=T_BLOCK):
    nq = seq // t
    k_top = min(TOPK_MAX, seq // 4)
    return pl.pallas_call(
        functools.partial(_dsa_select_kernel, t=t, k_top=k_top),
        grid=(batch, nq),
        in_specs=[
            pl.BlockSpec((IDX_PAIRS, t, LANE), lambda b, i: (QI_BLOCK0 // IDX_PAIRS, b * nq + i, 0)),
            pl.BlockSpec((1, seq, LANE), lambda b, i: (KI_BLOCK, b, 0)),
            pl.BlockSpec((1, t, LANE), lambda b, i: (0, b * nq + i, 0)),
        ],
        out_specs=pl.BlockSpec((1, 1, nq, t, t), lambda b, i: (b, i, 0, 0, 0)),
        out_shape=jax.ShapeDtypeStruct((batch, nq, nq, t, t), BF16),
        scratch_shapes=[
            pltpu.VMEM((nq, t, t), jnp.int32),
            pltpu.VMEM((nq, t, t), jnp.int16),
            pltpu.VMEM((nq, t, t), jnp.int16),
            pltpu.VMEM((t, LANE), jnp.int32),
            pltpu.VMEM((t, LANE), jnp.int32),
            pltpu.VMEM((t, LANE), jnp.int16),
            pltpu.VMEM((t, LANE), F32),
        ],
        compiler_params=_params(2),
        name="dsa_indexer_select",
    )(proj, proj, wi)


def _rope_tables(seq):
    inv = 1.0 / (ROPE_THETA ** (jnp.arange(0, HEAD_DIM, 2, dtype=F32) / HEAD_DIM))
    ang = jnp.arange(seq, dtype=F32)[:, None] * inv[None, :]
    cos, sin = jnp.cos(ang), jnp.sin(ang)
    zero = jnp.zeros_like(sin)
    n_heads = LANE // HEAD_DIM
    cos_t = jnp.tile(cos, (1, 2 * n_heads))
    sin_first = jnp.tile(jnp.concatenate([-sin, zero], axis=1), (1, n_heads))
    sin_second = jnp.tile(jnp.concatenate([zero, sin], axis=1), (1, n_heads))
    return cos_t, sin_first, sin_second


def _stream_ops(q_rope, k_rope, q_scale):
    q_ops = tuple((q_rope, q_scale, ("main", Q_BLOCK0 + p)) for p in range(N_PAIRS))
    k_ops = tuple((k_rope, 1.0, ("main", K_BLOCK0 + p)) for p in range(N_PAIRS))
    v_ops = tuple((False, 1.0, ("vt", p)) for p in range(N_PAIRS))
    g_ops = tuple((False, 1.0, ("main", G_BLOCK0 + p)) for p in range(N_PAIRS))
    return q_ops + k_ops + v_ops + g_ops


def _one_layer(layer, x2d, params, rope_tabs, batch, seq):
    d = x2d.shape[1]
    mixer, j = layer % N_MIXERS, layer // N_MIXERS
    main = 4 * INNER
    if mixer == 0:
        w = params["w_in_a"][j]
        w_idx_q = w[:, main:main + IDX_HEADS * IDX_DIM]
        w_idx_w = w[:, main + IDX_HEADS * IDX_DIM:main + IDX_HEADS * IDX_DIM + IDX_HEADS]
        w_idx_k = w[:, main + IDX_HEADS * IDX_DIM + IDX_HEADS:]
        w_pad = jnp.concatenate(
            [w[:, :main], w_idx_q, w_idx_k, w_idx_k, w_idx_w, jnp.zeros((d, LANE - IDX_HEADS), w.dtype)],
            axis=1).astype(BF16)
        ops = (_stream_ops(True, True, LOGIT_SCALE * LOG2E)
               + tuple((True, IDX_DIM ** -0.5, ("main", QI_BLOCK0 + p)) for p in range(IDX_PAIRS))
               + ((True, 1.0, ("main", KI_BLOCK)), (False, IDX_HEADS ** -0.5, ("f32", 0))))
        proj, vt, wi = _in_proj(x2d, w_pad, rope_tabs, ops, seq)
        bias = _dsa_select(proj, wi, batch, seq)
        o = _dsa_attention(proj, vt, bias, batch, seq)
        w_out = params["w_out_a"][j]
    elif mixer == 1:
        proj, vt, _ = _in_proj(x2d, params["w_in_b"][j].astype(BF16), rope_tabs, _stream_ops(False, False, LOGIT_SCALE), seq)
        o = _sb_attention(proj, vt, batch, seq)
        w_out = params["w_out_b"][j]
    else:
        lambda_init = 0.8 - 0.6 * math.exp(-0.3 * layer)
        proj, vt, _ = _in_proj(x2d, params["w_in_c"][j].astype(BF16), rope_tabs, _stream_ops(True, True, LOGIT_SCALE * LOG2E),
                               seq)
        lam_rows = jnp.stack([params["lambda_q1"][j], params["lambda_k1"][j],
                              params["lambda_q2"][j], params["lambda_k2"][j]]).astype(F32)
        lam_params = jnp.pad(lam_rows, ((0, 0), (0, LANE - DIFF_DIM)))
        sub_g = params["subln_g"][j].reshape(1, 2 * DIFF_DIM).astype(F32)
        o = _diff_attention(proj, vt, lam_params, sub_g, lambda_init, batch, seq)
        w_out = params["w_out_c"][j]
    return _out_proj_ln(o, proj, x2d, w_out.astype(BF16), params["ln_g"][layer], params["ln_b"][layer])


def kernel(x, w_in_a, w_out_a, w_in_b, w_out_b, w_in_c, w_out_c,
           lambda_q1, lambda_k1, lambda_q2, lambda_k2, subln_g, ln_g, ln_b):
    batch, seq, d = x.shape
    params = dict(w_in_a=w_in_a, w_out_a=w_out_a, w_in_b=w_in_b, w_out_b=w_out_b, w_in_c=w_in_c, w_out_c=w_out_c,
                  lambda_q1=lambda_q1, lambda_k1=lambda_k1, lambda_q2=lambda_q2, lambda_k2=lambda_k2,
                  subln_g=subln_g, ln_g=ln_g, ln_b=ln_b)
    x2d = x.reshape(batch * seq, d)
    rope_tabs = _rope_tables(seq)
    for layer in range(DEPTH):
        x2d = _one_layer(layer, x2d, params, rope_tabs, batch, seq)
    return x2d.reshape(batch, seq, d)
```

```python
import functools
import math

import jax
import jax.numpy as jnp
from jax import lax
from jax.experimental import pallas as pl
from jax.experimental.pallas import tpu as pltpu

F32 = jnp.float32
BF16 = jnp.bfloat16

LANE = 128
V7X_VMEM_LIMIT_BYTES = 56 * 1024 * 1024

D_MODEL = 1024
DEPTH = 4
CHUNK = 64
N_MIXERS = 3
N_HEADS = 16
HEAD_DIM = 64
INNER = N_HEADS * HEAD_DIM
N_PAIRS = INNER // LANE
IDX_HEADS = 8
IDX_DIM = 64
IDX_PAIRS = IDX_HEADS * IDX_DIM // LANE
TOPK_MAX = 256
DIFF_DIM = HEAD_DIM
ROPE_THETA = 10000.0
LN_EPS = 1e-5
RMS_EPS = 1e-5
ALPHA = (2.0 * DEPTH) ** 0.25
LOGIT_SCALE = HEAD_DIM ** -0.5
NEG_INF = float("-inf")
INT32_MIN = -2 ** 31
LOG2E = 1.4426950408889634
EXP_UNDERFLOW = -104.5

DENOM_ROWS = 16
T_BLOCK = 256
PAIRS_PER_STEP = 4

Q_BLOCK0, K_BLOCK0, G_BLOCK0, QI_BLOCK0 = 0, N_PAIRS, 2 * N_PAIRS, 3 * N_PAIRS
KI_BLOCK = QI_BLOCK0 + IDX_PAIRS

_NT = (((1,), (1,)), ((), ()))


def _params(n_grid_dims):
    return pltpu.CompilerParams(
        dimension_semantics=("arbitrary",) * n_grid_dims,
        vmem_limit_bytes=V7X_VMEM_LIMIT_BYTES,
    )


def _in_proj_kernel(x_ref, w_ref, cos_ref, sa_ref, sb_ref, main_ref, vt_ref, *f32_ref, ops, chunk, t):
    x = x_ref[...].astype(BF16)
    n_blocks = len(ops)
    tm = x.shape[0]
    for c0 in range(0, n_blocks, chunk):
        nb = min(chunk, n_blocks - c0)
        acc = jnp.dot(x, w_ref[:, c0 * LANE:(c0 + nb) * LANE], preferred_element_type=F32)
        for j in range(nb):
            rope, scale, (dest, slot) = ops[c0 + j]
            y = acc[:, j * LANE:(j + 1) * LANE]
            if rope:
                y = (y * cos_ref[...] + pltpu.roll(y, LANE - 32, 1) * sa_ref[...]
                     + pltpu.roll(y, 32, 1) * sb_ref[...])
            if scale != 1.0:
                y = y * scale
            if dest == "main":
                main_ref[slot] = y.astype(BF16)
            elif dest == "vt":
                y_t = y.T
                for kb in range(tm // t):
                    vt_ref[slot, kb] = y_t[:, kb * t:(kb + 1) * t].astype(BF16)
            else:
                f32_ref[0][slot] = y


def _in_proj(x2d, w, rope_tabs, ops, seq, tm=512, chunk=4):
    m, d = x2d.shape
    t = T_BLOCK
    assert w.shape == (d, len(ops) * LANE) and m % tm == 0 and seq % tm == 0 and tm % t == 0
    n_main = sum(1 for o in ops if o[2][0] == "main")
    n_f32 = sum(1 for o in ops if o[2][0] == "f32")
    s_tiles = seq // tm
    out_shape = [jax.ShapeDtypeStruct((n_main, m, LANE), BF16),
                 jax.ShapeDtypeStruct((N_PAIRS, m // t, LANE, t), BF16)]
    out_specs = [pl.BlockSpec((n_main, tm, LANE), lambda i: (0, i, 0)),
                 pl.BlockSpec((N_PAIRS, tm // t, LANE, t), lambda i: (0, i, 0, 0))]
    if n_f32:
        out_shape.append(jax.ShapeDtypeStruct((n_f32, m, LANE), F32))
        out_specs.append(pl.BlockSpec((n_f32, tm, LANE), lambda i: (0, i, 0)))
    tab_spec = pl.BlockSpec((tm, LANE), lambda i: (i % s_tiles, 0))
    res = pl.pallas_call(
        functools.partial(_in_proj_kernel, ops=ops, chunk=chunk, t=t),
        grid=(m // tm,),
        in_specs=[
            pl.BlockSpec((tm, d), lambda i: (i, 0)),
            pl.BlockSpec((d, len(ops) * LANE), lambda i: (0, 0)),
            tab_spec, tab_spec, tab_spec,
        ],
        out_specs=out_specs,
        out_shape=out_shape,
        compiler_params=_params(1),
        name="in_proj",
    )(x2d, w, *rope_tabs)
    return res if n_f32 else (res[0], res[1], None)


def _out_proj_ln_kernel(o_ref, g_ref, x_ref, w_ref, lng_ref, lnb_ref, out_ref):
    parts = []
    for p in range(N_PAIRS):
        g = g_ref[p].astype(F32)
        gate = g * (1.0 / (1.0 + jnp.exp(-g)))
        parts.append((o_ref[p] * gate).astype(BF16))
    og = jnp.concatenate(parts, axis=1)
    y = jnp.dot(og, w_ref[...], preferred_element_type=F32)
    z = ALPHA * x_ref[...] + y
    mu = jnp.mean(z, axis=-1, keepdims=True)
    zc = z - mu
    var = jnp.mean(zc * zc, axis=-1, keepdims=True)
    out_ref[...] = zc * lax.rsqrt(var + LN_EPS) * lng_ref[...] + lnb_ref[...]


def _out_proj_ln(o, proj, x2d, w_out, ln_g, ln_b, tm=512):
    m, d = x2d.shape
    return pl.pallas_call(
        _out_proj_ln_kernel,
        grid=(m // tm,),
        in_specs=[
            pl.BlockSpec((N_PAIRS, tm, LANE), lambda i: (0, i, 0)),
            pl.BlockSpec((N_PAIRS, tm, LANE), lambda i: (G_BLOCK0 // N_PAIRS, i, 0)),
            pl.BlockSpec((tm, d), lambda i: (i, 0)),
            pl.BlockSpec((INNER, d), lambda i: (0, 0)),
            pl.BlockSpec((1, d), lambda i: (0, 0)),
            pl.BlockSpec((1, d), lambda i: (0, 0)),
        ],
        out_specs=pl.BlockSpec((tm, d), lambda i: (i, 0)),
        out_shape=jax.ShapeDtypeStruct((m, d), F32),
        compiler_params=_params(1),
        name="out_proj_ln",
    )(o, proj, x2d, w_out, ln_g.reshape(1, d), ln_b.reshape(1, d))


def _head_query(q_pair, h):
    is_a = lax.broadcasted_iota(jnp.int32, (1, LANE), 1) < HEAD_DIM
    keep = is_a if h == 0 else jnp.logical_not(is_a)
    return jnp.where(keep, q_pair, jnp.zeros_like(q_pair))


def _key_query_iota(t):
    key = lax.broadcasted_iota(jnp.int32, (t, t), 0)
    qry = lax.broadcasted_iota(jnp.int32, (t, t), 1)
    return key, qry


def _attn_specs(n_q_blocks, seq, t, pg):
    q_spec = pl.BlockSpec((pg, t, LANE), lambda b, h, i: (Q_BLOCK0 // pg + h, b * n_q_blocks + i, 0))
    k_spec = pl.BlockSpec((pg, seq, LANE), lambda b, h, i: (K_BLOCK0 // pg + h, b, 0))
    vt_spec = pl.BlockSpec((pg, seq // t, LANE, t), lambda b, h, i: (h, b, 0, 0))
    o_spec = pl.BlockSpec((pg, t, LANE), lambda b, h, i: (h, b * n_q_blocks + i, 0))
    return q_spec, k_spec, vt_spec, o_spec


def _key_block(k_ref, p, kb, t):
    return k_ref[p, pl.ds(pl.multiple_of(kb * t, t), t), :]


def _sb_kernel(q_ref, k_ref, vt_ref, o_ref, acc_ref, carry_ref, *, t, pg):
    i = pl.program_id(2)
    key, qry = _key_query_iota(t)
    before = key < qry
    later_keys = jnp.where(qry > key, 1.0, 0.0).astype(BF16)
    acc_ref[...] = jnp.zeros_like(acc_ref)
    carry_ref[...] = jnp.zeros_like(carry_ref)

    def block_step(kb, diagonal):
        zs = [lax.dot_general(_key_block(k_ref, hh // 2, kb, t), _head_query(q_ref[hh // 2], hh % 2), _NT,
                              preferred_element_type=F32) for hh in range(2 * pg)]
        pre, cum = [], []
        for hh in range(2 * pg):
            z = zs[hh]
            soft = jnp.log(1.0 + jnp.exp2(jnp.abs(z) * -LOG2E))
            log_sig = jnp.minimum(z, 0.0) - soft
            log_om = log_sig - z
            if diagonal:
                log_om = jnp.where(before, log_om, 0.0)
            hi = log_om.astype(BF16)
            lo = (log_om - hi.astype(F32)).astype(BF16)
            c = (jnp.dot(later_keys, hi, preferred_element_type=F32)
                 + jnp.dot(later_keys, lo, preferred_element_type=F32))
            cum.append(c)
            carry = carry_ref[hh:hh + 1, :]
            pre.append(log_sig + carry)
            carry_ref[hh:hh + 1, :] = carry + c[0:1, :] + log_om[0:1, :]
        for hh in range(2 * pg):
            a = jnp.exp(pre[hh] + cum[hh])
            if diagonal:
                a = jnp.where(before, a, 0.0)
            h = hh % 2
            acc_ref[hh] += jnp.dot(vt_ref[hh // 2, kb, h * HEAD_DIM:(h + 1) * HEAD_DIM, :], a.astype(BF16),
                                   preferred_element_type=F32)

    def some_weight_left():
        return (jnp.max(carry_ref[...]) >= EXP_UNDERFLOW).astype(jnp.int32)

    block_step(i, True)

    def kb_cond(state):
        j, go = state
        return jnp.logical_and(j < i, go == 1)

    def kb_body(state):
        j, _ = state
        block_step(i - 1 - j, False)
        return j + 1, some_weight_left()

    lax.while_loop(kb_cond, kb_body, (jnp.int32(0), some_weight_left()))
    for p in range(pg):
        o_ref[p] = jnp.concatenate([acc_ref[2 * p], acc_ref[2 * p + 1]], axis=0).T


def _sb_attention(proj, vt, batch, seq, t=T_BLOCK, pg=PAIRS_PER_STEP):
    nq = seq // t
    q_spec, k_spec, vt_spec, o_spec = _attn_specs(nq, seq, t, pg)
    return pl.pallas_call(
        functools.partial(_sb_kernel, t=t, pg=pg),
        grid=(batch, N_PAIRS // pg, nq),
        in_specs=[q_spec, k_spec, vt_spec],
        out_specs=o_spec,
        out_shape=jax.ShapeDtypeStruct((N_PAIRS, batch * seq, LANE), F32),
        scratch_shapes=[pltpu.VMEM((2 * pg, HEAD_DIM, t), F32), pltpu.VMEM((2 * pg, t), F32)],
        compiler_params=_params(3),
        name="stick_breaking_attention",
    )(proj, proj, vt)


def _softmax_block(hh, s, s_max, vt_rows, m_ref, acc_ref):
    m_prev = m_ref[hh]
    m_new = jnp.maximum(m_prev, s_max)
    m_safe = jnp.where(m_new == NEG_INF, 0.0, m_new)
    alpha = jnp.exp2(m_prev - m_safe)
    p = jnp.exp2(s - m_safe).astype(BF16)
    lhs = jnp.concatenate([vt_rows, jnp.ones((DENOM_ROWS, vt_rows.shape[1]), BF16)], axis=0)
    acc_ref[hh] = alpha * acc_ref[hh] + jnp.dot(lhs, p, preferred_element_type=F32)
    m_ref[hh] = m_new


def _reset_softmax_state(m_ref, acc_ref):
    m_ref[...] = jnp.full(m_ref.shape, NEG_INF, F32)
    acc_ref[...] = jnp.zeros_like(acc_ref)


def _normalized(acc_ref, hh, n_rows):
    acc = acc_ref[hh]
    return acc[:n_rows] / acc[n_rows:n_rows + 1]


def _score_tiles_into(s_ref, smax_ref, q_ref, k_ref, t, pg, bias_of, slot, kb):
    bias = None if bias_of is None else bias_of(kb)
    for hh in range(2 * pg):
        s = lax.dot_general(_key_block(k_ref, hh // 2, kb, t), _head_query(q_ref[hh // 2], hh % 2), _NT,
                            preferred_element_type=F32)
        if bias is not None:
            s = s + bias
        s_ref[slot, hh] = s
        smax_ref[slot, hh] = jnp.max(s, axis=0, keepdims=True)


def _pipelined_key_loop(i, scores_into, consume):
    scores_into(0, 0)

    def pair_body(jj, _):
        kb = 2 * jj
        scores_into(1, kb + 1)
        consume(0, kb, False)
        scores_into(0, kb + 2)
        consume(1, kb + 1, False)
        return 0

    lax.fori_loop(0, i // 2, pair_body, 0)

    @pl.when(i % 2 == 1)
    def _():
        scores_into(1, i)
        consume(0, i - 1, False)

    consume(i % 2, i, True)


def _diff_kernel(lam_ref, subg_ref, q_ref, k_ref, vt_ref, o_ref, m_ref, acc_ref, s_ref, smax_ref, *, t, pg,
                 lambda_init):
    i = pl.program_id(2)
    key, qry = _key_query_iota(t)
    adm = (key // CHUNK) <= (qry // CHUNK)
    _reset_softmax_state(m_ref, acc_ref)

    def consume(slot, kb, diagonal):
        for hh in range(2 * pg):
            s = s_ref[slot, hh]
            if diagonal:
                s = jnp.where(adm, s, NEG_INF)
                s_max = jnp.max(s, axis=0, keepdims=True)
            else:
                s_max = smax_ref[slot, hh]
            _softmax_block(hh, s, s_max, vt_ref[hh // 2, kb], m_ref, acc_ref)

    _pipelined_key_loop(i, functools.partial(_score_tiles_into, s_ref, smax_ref, q_ref, k_ref, t, pg, None),
                        consume)

    lp = lam_ref[...]
    lam = (jnp.exp(jnp.sum(lp[0:1] * lp[1:2], axis=-1, keepdims=True))
           - jnp.exp(jnp.sum(lp[2:3] * lp[3:4], axis=-1, keepdims=True)) + lambda_init)
    for p in range(pg):
        d = _normalized(acc_ref, 2 * p, LANE) - lam * _normalized(acc_ref, 2 * p + 1, LANE)
        ms = jnp.mean(d * d, axis=0, keepdims=True)
        o_ref[p] = (d * lax.rsqrt(ms + RMS_EPS)).T * subg_ref[...] * (1.0 - lambda_init)


def _diff_attention(proj, vt, lam_params, sub_g, lambda_init, batch, seq, t=T_BLOCK, pg=PAIRS_PER_STEP):
    nq = seq // t
    q_spec, k_spec, vt_spec, o_spec = _attn_specs(nq, seq, t, pg)
    return pl.pallas_call(
        functools.partial(_diff_kernel, t=t, pg=pg, lambda_init=lambda_init),
        grid=(batch, N_PAIRS // pg, nq),
        in_specs=[
            pl.BlockSpec((4, LANE), lambda b, h, i: (0, 0)),
            pl.BlockSpec((1, LANE), lambda b, h, i: (0, 0)),
            q_spec, k_spec, vt_spec,
        ],
        out_specs=o_spec,
        out_shape=jax.ShapeDtypeStruct((N_PAIRS, batch * seq, LANE), F32),
        scratch_shapes=[pltpu.VMEM((2 * pg, 1, t), F32), pltpu.VMEM((2 * pg, LANE + DENOM_ROWS, t), F32),
                        pltpu.VMEM((2, 2 * pg, t, t), F32), pltpu.VMEM((2, 2 * pg, 1, t), F32)],
        compiler_params=_params(3),
        name="differential_attention",
    )(lam_params, sub_g, proj, proj, vt)


def _dsa_attn_kernel(bias_ref, q_ref, k_ref, vt_ref, o_ref, m_ref, acc_ref, s_ref, smax_ref, *, t, pg):
    i = pl.program_id(2)
    _reset_softmax_state(m_ref, acc_ref)

    def consume(slot, kb, diagonal):
        del diagonal
        for hh in range(2 * pg):
            h = hh % 2
            _softmax_block(hh, s_ref[slot, hh], smax_ref[slot, hh],
                           vt_ref[hh // 2, kb, h * HEAD_DIM:(h + 1) * HEAD_DIM, :], m_ref, acc_ref)

    def bias_of(kb):
        return bias_ref[0, 0, kb].astype(F32)

    _pipelined_key_loop(i, functools.partial(_score_tiles_into, s_ref, smax_ref, q_ref, k_ref, t, pg, bias_of),
                        consume)
    for p in range(pg):
        o_ref[p] = jnp.concatenate([_normalized(acc_ref, 2 * p, HEAD_DIM),
                                    _normalized(acc_ref, 2 * p + 1, HEAD_DIM)], axis=0).T


def _dsa_attention(proj, vt, bias, batch, seq, t=T_BLOCK, pg=PAIRS_PER_STEP):
    nq = seq // t
    q_spec, k_spec, vt_spec, o_spec = _attn_specs(nq, seq, t, pg)
    return pl.pallas_call(
        functools.partial(_dsa_attn_kernel, t=t, pg=pg),
        grid=(batch, N_PAIRS // pg, nq),
        in_specs=[
            pl.BlockSpec((1, 1, nq, t, t), lambda b, h, i: (b, i, 0, 0, 0)),
            q_spec, k_spec, vt_spec,
        ],
        out_specs=o_spec,
        out_shape=jax.ShapeDtypeStruct((N_PAIRS, batch * seq, LANE), F32),
        scratch_shapes=[pltpu.VMEM((2 * pg, 1, t), F32), pltpu.VMEM((2 * pg, HEAD_DIM + DENOM_ROWS, t), F32),
                        pltpu.VMEM((2, 2 * pg, t, t), F32), pltpu.VMEM((2, 2 * pg, 1, t), F32)],
        compiler_params=_params(3),
        name="dsa_sparse_attention",
    )(bias, proj, proj, vt)


def _dsa_select_kernel(qi_ref, ki_ref, wi_ref, bias_ref, key_ref, thr_ref, nge_ref, cnt_ref, run_ref, *, t, k_top):
    i = pl.program_id(1)
    nq = bias_ref.shape[2]
    n_groups = t // LANE
    r = lax.broadcasted_iota(jnp.int32, (t, t), 0)
    c = lax.broadcasted_iota(jnp.int32, (t, t), 1)
    adm = (c // CHUNK) <= (r // CHUNK)
    earlier_keys = jnp.where(r < c, 1.0, 0.0).astype(BF16)
    wi = wi_ref[0]

    def score_block(kb, diagonal):
        ks = ki_ref[0, pl.ds(pl.multiple_of(kb * t, t), t), :]
        sc = jnp.zeros((t, t), F32)
        for pr in range(IDX_PAIRS):
            for h in range(2):
                rel = jnp.maximum(
                    lax.dot_general(_head_query(qi_ref[pr], h), ks, _NT, preferred_element_type=F32), 0.0)
                hh = 2 * pr + h
                sc = sc + rel * wi[:, hh:hh + 1]
        if diagonal:
            sc = jnp.where(adm, sc, NEG_INF)
        bits = pltpu.bitcast(sc, jnp.int32)
        srt = bits ^ ((bits >> 31) & 0x7FFFFFFF)
        key_ref[kb] = jnp.where(sc == 0.0, 0, srt)

    def score_body(kb, _):
        score_block(kb, False)
        return 0

    lax.fori_loop(0, i, score_body, 0)
    score_block(i, True)

    def count_where(pred):
        cnt_ref[...] = jnp.zeros_like(cnt_ref)

        def body(kb, _):
            kk = key_ref[kb]
            part = cnt_ref[...]
            for g in range(n_groups):
                part = part + jnp.where(pred(kk[:, g * LANE:(g + 1) * LANE]), 1.0, 0.0)
            cnt_ref[...] = part
            return 0

        lax.fori_loop(0, i + 1, body, 0)
        return jnp.sum(cnt_ref[...], axis=-1, keepdims=True)

    thr_ref[...] = jnp.full(thr_ref.shape, INT32_MIN, jnp.int32)
    nge_ref[...] = jnp.full(nge_ref.shape, float(t), F32) * (i + 1).astype(F32)

    def bit_cond(state):
        it, settled = state
        return jnp.logical_and(it < 32, settled == 0)

    def bit_body(state):
        it, _ = state
        cand = thr_ref[...] + jnp.left_shift(jnp.int32(1), 31 - it)
        n_ge = count_where(lambda kk: kk >= cand)
        accept = n_ge >= float(k_top)
        thr_ref[...] = jnp.where(accept, cand, thr_ref[...])
        nge = jnp.where(accept, n_ge, nge_ref[...])
        nge_ref[...] = nge
        return it + 1, (jnp.max(nge) == float(k_top)).astype(jnp.int32)

    lax.while_loop(bit_cond, bit_body, (jnp.int32(0), jnp.int32(0)))
    thr = thr_ref[...]

    surplus_ties = jnp.max(nge_ref[...]) > float(k_top)

    def plain_block(kb, diagonal):
        kk = key_ref[kb]
        bias = jnp.concatenate(
            [jnp.where(kk[:, g * LANE:(g + 1) * LANE] >= thr, 0.0, NEG_INF) for g in range(n_groups)], axis=1)
        if diagonal:
            bias = jnp.where(adm, bias, NEG_INF)
        bias_ref[0, 0, kb] = bias.T.astype(BF16)

    @pl.when(jnp.logical_not(surplus_ties))
    def _():
        def plain_body(kb, _):
            plain_block(kb, False)
            return 0

        lax.fori_loop(0, i, plain_body, 0)
        plain_block(i, True)

    def tie_block(kb, diagonal, need):
        kk = key_ref[kb]
        eq = jnp.concatenate(
            [jnp.where(kk[:, g * LANE:(g + 1) * LANE] == thr, 1.0, 0.0) for g in range(n_groups)], axis=1)
        rank = jnp.dot(eq.astype(BF16), earlier_keys, preferred_element_type=F32)
        parts = []
        for g in range(n_groups):
            sl = slice(g * LANE, (g + 1) * LANE)
            tie_ok = (rank[:, sl] + run_ref[...]) < need
            parts.append(jnp.where(kk[:, sl] > thr, 0.0,
                                   jnp.where(kk[:, sl] == thr, jnp.where(tie_ok, 0.0, NEG_INF), NEG_INF)))
        bias = jnp.concatenate(parts, axis=1)
        if diagonal:
            bias = jnp.where(adm, bias, NEG_INF)
        bias_ref[0, 0, kb] = bias.T.astype(BF16)
        run_ref[...] = run_ref[...] + jnp.sum(eq, axis=-1, keepdims=True)

    @pl.when(surplus_ties)
    def _():
        need = float(k_top) - count_where(lambda kk: kk > thr)
        run_ref[...] = jnp.zeros_like(run_ref)

        def tie_body(kb, _):
            tie_block(kb, False, need)
            return 0

        lax.fori_loop(0, i, tie_body, 0)
        tie_block(i, True, need)

    def fill_body(kb, _):
        bias_ref[0, 0, kb] = jnp.full((t, t), NEG_INF, BF16)
        return 0

    lax.fori_loop(i + 1, nq, fill_body, 0)


def _dsa_select(proj, wi, batch, seq, t=T_BLOCK):
    nq = seq // t
    k_top = min(TOPK_MAX, seq // 4)
    return pl.pallas_call(
        functools.partial(_dsa_select_kernel, t=t, k_top=k_top),
        grid=(batch, nq),
        in_specs=[
            pl.BlockSpec((IDX_PAIRS, t, LANE), lambda b, i: (QI_BLOCK0 // IDX_PAIRS, b * nq + i, 0)),
            pl.BlockSpec((1, seq, LANE), lambda b, i: (KI_BLOCK, b, 0)),
            pl.BlockSpec((1, t, LANE), lambda b, i: (0, b * nq + i, 0)),
        ],
        out_specs=pl.BlockSpec((1, 1, nq, t, t), lambda b, i: (b, i, 0, 0, 0)),
        out_shape=jax.ShapeDtypeStruct((batch, nq, nq, t, t), BF16),
        scratch_shapes=[
            pltpu.VMEM((nq, t, t), jnp.int32),
            pltpu.VMEM((t, LANE), jnp.int32),
            pltpu.VMEM((t, LANE), F32),
            pltpu.VMEM((t, LANE), F32),
            pltpu.VMEM((t, LANE), F32),
        ],
        compiler_params=_params(2),
        name="dsa_indexer_select",
    )(proj, proj, wi)


def _rope_tables(seq):
    inv = 1.0 / (ROPE_THETA ** (jnp.arange(0, HEAD_DIM, 2, dtype=F32) / HEAD_DIM))
    ang = jnp.arange(seq, dtype=F32)[:, None] * inv[None, :]
    cos, sin = jnp.cos(ang), jnp.sin(ang)
    zero = jnp.zeros_like(sin)
    n_heads = LANE // HEAD_DIM
    cos_t = jnp.tile(cos, (1, 2 * n_heads))
    sin_first = jnp.tile(jnp.concatenate([-sin, zero], axis=1), (1, n_heads))
    sin_second = jnp.tile(jnp.concatenate([zero, sin], axis=1), (1, n_heads))
    return cos_t, sin_first, sin_second


def _stream_ops(q_rope, k_rope, q_scale):
    q_ops = tuple((q_rope, q_scale, ("main", Q_BLOCK0 + p)) for p in range(N_PAIRS))
    k_ops = tuple((k_rope, 1.0, ("main", K_BLOCK0 + p)) for p in range(N_PAIRS))
    v_ops = tuple((False, 1.0, ("vt", p)) for p in range(N_PAIRS))
    g_ops = tuple((False, 1.0, ("main", G_BLOCK0 + p)) for p in range(N_PAIRS))
    return q_ops + k_ops + v_ops + g_ops


def _one_layer(layer, x2d, params, rope_tabs, batch, seq):
    d = x2d.shape[1]
    mixer, j = layer % N_MIXERS, layer // N_MIXERS
    main = 4 * INNER
    if mixer == 0:
        w = params["w_in_a"][j]
        w_idx_q = w[:, main:main + IDX_HEADS * IDX_DIM]
        w_idx_w = w[:, main + IDX_HEADS * IDX_DIM:main + IDX_HEADS * IDX_DIM + IDX_HEADS]
        w_idx_k = w[:, main + IDX_HEADS * IDX_DIM + IDX_HEADS:]
        w_pad = jnp.concatenate(
            [w[:, :main], w_idx_q, w_idx_k, w_idx_k, w_idx_w, jnp.zeros((d, LANE - IDX_HEADS), w.dtype)],
            axis=1).astype(BF16)
        ops = (_stream_ops(True, True, LOGIT_SCALE * LOG2E)
               + tuple((True, IDX_DIM ** -0.5, ("main", QI_BLOCK0 + p)) for p in range(IDX_PAIRS))
               + ((True, 1.0, ("main", KI_BLOCK)), (False, IDX_HEADS ** -0.5, ("f32", 0))))
        proj, vt, wi = _in_proj(x2d, w_pad, rope_tabs, ops, seq)
        bias = _dsa_select(proj, wi, batch, seq)
        o = _dsa_attention(proj, vt, bias, batch, seq)
        w_out = params["w_out_a"][j]
    elif mixer == 1:
        proj, vt, _ = _in_proj(x2d, params["w_in_b"][j].astype(BF16), rope_tabs, _stream_ops(False, False, LOGIT_SCALE), seq)
        o = _sb_attention(proj, vt, batch, seq)
        w_out = params["w_out_b"][j]
    else:
        lambda_init = 0.8 - 0.6 * math.exp(-0.3 * layer)
        proj, vt, _ = _in_proj(x2d, params["w_in_c"][j].astype(BF16), rope_tabs, _stream_ops(True, True, LOGIT_SCALE * LOG2E),
                               seq)
        lam_rows = jnp.stack([params["lambda_q1"][j], params["lambda_k1"][j],
                              params["lambda_q2"][j], params["lambda_k2"][j]]).astype(F32)
        lam_params = jnp.pad(lam_rows, ((0, 0), (0, LANE - DIFF_DIM)))
        sub_g = params["subln_g"][j].reshape(1, 2 * DIFF_DIM).astype(F32)
        o = _diff_attention(proj, vt, lam_params, sub_g, lambda_init, batch, seq)
        w_out = params["w_out_c"][j]
    return _out_proj_ln(o, proj, x2d, w_out.astype(BF16), params["ln_g"][layer], params["ln_b"][layer])


def kernel(x, w_in_a, w_out_a, w_in_b, w_out_b, w_in_c, w_out_c,
           lambda_q1, lambda_k1, lambda_q2, lambda_k2, subln_g, ln_g, ln_b):
    batch, seq, d = x.shape
    params = dict(w_in_a=w_in_a, w_out_a=w_out_a, w_in_b=w_in_b, w_out_b=w_out_b, w_in_c=w_in_c, w_out_c=w_out_c,
                  lambda_q1=lambda_q1, lambda_k1=lambda_k1, lambda_q2=lambda_q2, lambda_k2=lambda_k2,
                  subln_g=subln_g, ln_g=ln_g, ln_b=ln_b)
    x2d = x.reshape(batch * seq, d)
    rope_tabs = _rope_tables(seq)
    for layer in range(DEPTH):
        x2d = _one_layer(layer, x2d, params, rope_tabs, batch, seq)
    return x2d.reshape(batch, seq, d)
```

```python
import functools
import math

import jax
import jax.numpy as jnp
from jax import lax
from jax.experimental import pallas as pl
from jax.experimental.pallas import tpu as pltpu

F32 = jnp.float32
BF16 = jnp.bfloat16

LANE = 128
V7X_VMEM_LIMIT_BYTES = 56 * 1024 * 1024

D_MODEL = 1024
DEPTH = 4
CHUNK = 64
N_MIXERS = 3
N_HEADS = 16
HEAD_DIM = 64
INNER = N_HEADS * HEAD_DIM
N_PAIRS = INNER // LANE
IDX_HEADS = 8
IDX_DIM = 64
IDX_PAIRS = IDX_HEADS * IDX_DIM // LANE
TOPK_MAX = 256
DIFF_DIM = HEAD_DIM
ROPE_THETA = 10000.0
LN_EPS = 1e-5
RMS_EPS = 1e-5
ALPHA = (2.0 * DEPTH) ** 0.25
LOGIT_SCALE = HEAD_DIM ** -0.5
NEG_INF = float("-inf")
INT32_MIN = -2 ** 31
BRACKET = 4 << 23
BRACKET_STEPS = 26
LOG2E = 1.4426950408889634
EXP_UNDERFLOW = -104.5

DENOM_ROWS = 16
T_BLOCK = 256
PAIRS_PER_STEP = 4

Q_BLOCK0, K_BLOCK0, G_BLOCK0, QI_BLOCK0 = 0, N_PAIRS, 2 * N_PAIRS, 3 * N_PAIRS
KI_BLOCK = QI_BLOCK0 + IDX_PAIRS

_NT = (((1,), (1,)), ((), ()))


def _params(n_grid_dims):
    return pltpu.CompilerParams(
        dimension_semantics=("arbitrary",) * n_grid_dims,
        vmem_limit_bytes=V7X_VMEM_LIMIT_BYTES,
    )


def _in_proj_kernel(x_ref, w_ref, cos_ref, sa_ref, sb_ref, main_ref, vt_ref, *f32_ref, ops, chunk, t):
    x = x_ref[...].astype(BF16)
    n_blocks = len(ops)
    tm = x.shape[0]
    for c0 in range(0, n_blocks, chunk):
        nb = min(chunk, n_blocks - c0)
        acc = jnp.dot(x, w_ref[:, c0 * LANE:(c0 + nb) * LANE], preferred_element_type=F32)
        for j in range(nb):
            rope, scale, (dest, slot) = ops[c0 + j]
            y = acc[:, j * LANE:(j + 1) * LANE]
            if rope:
                y = (y * cos_ref[...] + pltpu.roll(y, LANE - 32, 1) * sa_ref[...]
                     + pltpu.roll(y, 32, 1) * sb_ref[...])
            if scale != 1.0:
                y = y * scale
            if dest == "main":
                main_ref[slot] = y.astype(BF16)
            elif dest == "vt":
                y_t = y.T
                for kb in range(tm // t):
                    vt_ref[slot, kb] = y_t[:, kb * t:(kb + 1) * t].astype(BF16)
            else:
                f32_ref[0][slot] = y


def _in_proj(x2d, w, rope_tabs, ops, seq, tm=512, chunk=4):
    m, d = x2d.shape
    t = T_BLOCK
    assert w.shape == (d, len(ops) * LANE) and m % tm == 0 and seq % tm == 0 and tm % t == 0
    n_main = sum(1 for o in ops if o[2][0] == "main")
    n_f32 = sum(1 for o in ops if o[2][0] == "f32")
    s_tiles = seq // tm
    out_shape = [jax.ShapeDtypeStruct((n_main, m, LANE), BF16),
                 jax.ShapeDtypeStruct((N_PAIRS, m // t, LANE, t), BF16)]
    out_specs = [pl.BlockSpec((n_main, tm, LANE), lambda i: (0, i, 0)),
                 pl.BlockSpec((N_PAIRS, tm // t, LANE, t), lambda i: (0, i, 0, 0))]
    if n_f32:
        out_shape.append(jax.ShapeDtypeStruct((n_f32, m, LANE), F32))
        out_specs.append(pl.BlockSpec((n_f32, tm, LANE), lambda i: (0, i, 0)))
    tab_spec = pl.BlockSpec((tm, LANE), lambda i: (i % s_tiles, 0))
    res = pl.pallas_call(
        functools.partial(_in_proj_kernel, ops=ops, chunk=chunk, t=t),
        grid=(m // tm,),
        in_specs=[
            pl.BlockSpec((tm, d), lambda i: (i, 0)),
            pl.BlockSpec((d, len(ops) * LANE), lambda i: (0, 0)),
            tab_spec, tab_spec, tab_spec,
        ],
        out_specs=out_specs,
        out_shape=out_shape,
        compiler_params=_params(1),
        name="in_proj",
    )(x2d, w, *rope_tabs)
    return res if n_f32 else (res[0], res[1], None)


def _out_proj_ln_kernel(o_ref, g_ref, x_ref, w_ref, lng_ref, lnb_ref, out_ref):
    parts = []
    for p in range(N_PAIRS):
        g = g_ref[p].astype(F32)
        gate = g * (1.0 / (1.0 + jnp.exp(-g)))
        parts.append((o_ref[p] * gate).astype(BF16))
    og = jnp.concatenate(parts, axis=1)
    y = jnp.dot(og, w_ref[...], preferred_element_type=F32)
    z = ALPHA * x_ref[...] + y
    mu = jnp.mean(z, axis=-1, keepdims=True)
    zc = z - mu
    var = jnp.mean(zc * zc, axis=-1, keepdims=True)
    out_ref[...] = zc * lax.rsqrt(var + LN_EPS) * lng_ref[...] + lnb_ref[...]


def _out_proj_ln(o, proj, x2d, w_out, ln_g, ln_b, tm=512):
    m, d = x2d.shape
    return pl.pallas_call(
        _out_proj_ln_kernel,
        grid=(m // tm,),
        in_specs=[
            pl.BlockSpec((N_PAIRS, tm, LANE), lambda i: (0, i, 0)),
            pl.BlockSpec((N_PAIRS, tm, LANE), lambda i: (G_BLOCK0 // N_PAIRS, i, 0)),
            pl.BlockSpec((tm, d), lambda i: (i, 0)),
            pl.BlockSpec((INNER, d), lambda i: (0, 0)),
            pl.BlockSpec((1, d), lambda i: (0, 0)),
            pl.BlockSpec((1, d), lambda i: (0, 0)),
        ],
        out_specs=pl.BlockSpec((tm, d), lambda i: (i, 0)),
        out_shape=jax.ShapeDtypeStruct((m, d), F32),
        compiler_params=_params(1),
        name="out_proj_ln",
    )(o, proj, x2d, w_out, ln_g.reshape(1, d), ln_b.reshape(1, d))


def _head_query(q_pair, h):
    is_a = lax.broadcasted_iota(jnp.int32, (1, LANE), 1) < HEAD_DIM
    keep = is_a if h == 0 else jnp.logical_not(is_a)
    return jnp.where(keep, q_pair, jnp.zeros_like(q_pair))


def _key_query_iota(t):
    key = lax.broadcasted_iota(jnp.int32, (t, t), 0)
    qry = lax.broadcasted_iota(jnp.int32, (t, t), 1)
    return key, qry


def _attn_specs(n_q_blocks, seq, t, pg):
    q_spec = pl.BlockSpec((pg, t, LANE), lambda b, h, i: (Q_BLOCK0 // pg + h, b * n_q_blocks + i, 0))
    k_spec = pl.BlockSpec((pg, seq, LANE), lambda b, h, i: (K_BLOCK0 // pg + h, b, 0))
    vt_spec = pl.BlockSpec((pg, seq // t, LANE, t), lambda b, h, i: (h, b, 0, 0))
    o_spec = pl.BlockSpec((pg, t, LANE), lambda b, h, i: (h, b * n_q_blocks + i, 0))
    return q_spec, k_spec, vt_spec, o_spec


def _key_block(k_ref, p, kb, t):
    return k_ref[p, pl.ds(pl.multiple_of(kb * t, t), t), :]


def _split_query_heads(q_ref, qh_ref, n_pairs):
    for p in range(n_pairs):
        for h in range(2):
            qh_ref[2 * p + h] = _head_query(q_ref[p], h)


def _sb_kernel(q_ref, k_ref, vt_ref, o_ref, acc_ref, carry_ref, qh_ref, *, t, pg):
    i = pl.program_id(2)
    _split_query_heads(q_ref, qh_ref, pg)
    key, qry = _key_query_iota(t)
    before = key < qry
    later_keys = jnp.where(qry > key, 1.0, 0.0).astype(BF16)
    acc_ref[...] = jnp.zeros_like(acc_ref)
    carry_ref[...] = jnp.zeros_like(carry_ref)

    def block_step(kb, diagonal):
        zs = [lax.dot_general(_key_block(k_ref, hh // 2, kb, t), qh_ref[hh], _NT, preferred_element_type=F32)
              for hh in range(2 * pg)]
        pre, cum = [], []
        for hh in range(2 * pg):
            z = zs[hh]
            soft = jnp.log(1.0 + jnp.exp2(jnp.abs(z) * -LOG2E))
            log_sig = jnp.minimum(z, 0.0) - soft
            log_om = log_sig - z
            if diagonal:
                log_om = jnp.where(before, log_om, 0.0)
            hi = log_om.astype(BF16)
            lo = (log_om - hi.astype(F32)).astype(BF16)
            c = (jnp.dot(later_keys, hi, preferred_element_type=F32)
                 + jnp.dot(later_keys, lo, preferred_element_type=F32))
            cum.append(c)
            carry = carry_ref[hh:hh + 1, :]
            pre.append(log_sig + carry)
            carry_ref[hh:hh + 1, :] = carry + c[0:1, :] + log_om[0:1, :]
        for hh in range(2 * pg):
            a = jnp.exp(pre[hh] + cum[hh])
            if diagonal:
                a = jnp.where(before, a, 0.0)
            h = hh % 2
            acc_ref[hh] += jnp.dot(vt_ref[hh // 2, kb, h * HEAD_DIM:(h + 1) * HEAD_DIM, :], a.astype(BF16),
                                   preferred_element_type=F32)

    def some_weight_left():
        return (jnp.max(carry_ref[...]) >= EXP_UNDERFLOW).astype(jnp.int32)

    block_step(i, True)

    def kb_cond(state):
        j, go = state
        return jnp.logical_and(j < i, go == 1)

    def kb_body(state):
        j, _ = state
        block_step(i - 1 - j, False)
        return j + 1, some_weight_left()

    lax.while_loop(kb_cond, kb_body, (jnp.int32(0), some_weight_left()))
    for p in range(pg):
        o_ref[p] = jnp.concatenate([acc_ref[2 * p], acc_ref[2 * p + 1]], axis=0).T


def _sb_attention(proj, vt, batch, seq, t=T_BLOCK, pg=PAIRS_PER_STEP):
    nq = seq // t
    q_spec, k_spec, vt_spec, o_spec = _attn_specs(nq, seq, t, pg)
    return pl.pallas_call(
        functools.partial(_sb_kernel, t=t, pg=pg),
        grid=(batch, N_PAIRS // pg, nq),
        in_specs=[q_spec, k_spec, vt_spec],
        out_specs=o_spec,
        out_shape=jax.ShapeDtypeStruct((N_PAIRS, batch * seq, LANE), F32),
        scratch_shapes=[pltpu.VMEM((2 * pg, HEAD_DIM, t), F32), pltpu.VMEM((2 * pg, t), F32),
                        pltpu.VMEM((2 * pg, t, LANE), BF16)],
        compiler_params=_params(3),
        name="stick_breaking_attention",
    )(proj, proj, vt)


def _softmax_block(hh, s, s_max, vt_rows, m_ref, acc_ref):
    m_prev = m_ref[hh]
    m_new = jnp.maximum(m_prev, s_max)
    m_safe = jnp.where(m_new == NEG_INF, 0.0, m_new)
    alpha = jnp.exp2(m_prev - m_safe)
    p = jnp.exp2(s - m_safe).astype(BF16)
    lhs = jnp.concatenate([vt_rows, jnp.ones((DENOM_ROWS, vt_rows.shape[1]), BF16)], axis=0)
    acc_ref[hh] = alpha * acc_ref[hh] + jnp.dot(lhs, p, preferred_element_type=F32)
    m_ref[hh] = m_new


def _reset_softmax_state(m_ref, acc_ref):
    m_ref[...] = jnp.full(m_ref.shape, NEG_INF, F32)
    acc_ref[...] = jnp.zeros_like(acc_ref)


def _normalized(acc_ref, hh, n_rows):
    acc = acc_ref[hh]
    return acc[:n_rows] / acc[n_rows:n_rows + 1]


def _score_tiles_into(s_ref, smax_ref, qh_ref, k_ref, t, pg, bias_of, slot, kb):
    bias = None if bias_of is None else bias_of(kb)
    for hh in range(2 * pg):
        s = lax.dot_general(_key_block(k_ref, hh // 2, kb, t), qh_ref[hh], _NT, preferred_element_type=F32)
        if bias is not None:
            s = s + bias
        s_ref[slot, hh] = s
        smax_ref[slot, hh] = jnp.max(s, axis=0, keepdims=True)


def _pipelined_key_loop(i, scores_into, consume):
    scores_into(0, 0)

    def pair_body(jj, _):
        kb = 2 * jj
        scores_into(1, kb + 1)
        consume(0, kb, False)
        scores_into(0, kb + 2)
        consume(1, kb + 1, False)
        return 0

    lax.fori_loop(0, i // 2, pair_body, 0)

    @pl.when(i % 2 == 1)
    def _():
        scores_into(1, i)
        consume(0, i - 1, False)

    consume(i % 2, i, True)


def _diff_kernel(lam_ref, subg_ref, q_ref, k_ref, vt_ref, o_ref, m_ref, acc_ref, s_ref, smax_ref, qh_ref, *, t, pg,
                 lambda_init):
    i = pl.program_id(2)
    _split_query_heads(q_ref, qh_ref, pg)
    key, qry = _key_query_iota(t)
    adm = (key // CHUNK) <= (qry // CHUNK)
    _reset_softmax_state(m_ref, acc_ref)

    def consume(slot, kb, diagonal):
        for hh in range(2 * pg):
            s = s_ref[slot, hh]
            if diagonal:
                s = jnp.where(adm, s, NEG_INF)
                s_max = jnp.max(s, axis=0, keepdims=True)
            else:
                s_max = smax_ref[slot, hh]
            _softmax_block(hh, s, s_max, vt_ref[hh // 2, kb], m_ref, acc_ref)

    _pipelined_key_loop(i, functools.partial(_score_tiles_into, s_ref, smax_ref, qh_ref, k_ref, t, pg, None),
                        consume)

    lp = lam_ref[...]
    lam = (jnp.exp(jnp.sum(lp[0:1] * lp[1:2], axis=-1, keepdims=True))
           - jnp.exp(jnp.sum(lp[2:3] * lp[3:4], axis=-1, keepdims=True)) + lambda_init)
    for p in range(pg):
        d = _normalized(acc_ref, 2 * p, LANE) - lam * _normalized(acc_ref, 2 * p + 1, LANE)
        ms = jnp.mean(d * d, axis=0, keepdims=True)
        o_ref[p] = (d * lax.rsqrt(ms + RMS_EPS)).T * subg_ref[...] * (1.0 - lambda_init)


def _diff_attention(proj, vt, lam_params, sub_g, lambda_init, batch, seq, t=T_BLOCK, pg=PAIRS_PER_STEP):
    nq = seq // t
    q_spec, k_spec, vt_spec, o_spec = _attn_specs(nq, seq, t, pg)
    return pl.pallas_call(
        functools.partial(_diff_kernel, t=t, pg=pg, lambda_init=lambda_init),
        grid=(batch, N_PAIRS // pg, nq),
        in_specs=[
            pl.BlockSpec((4, LANE), lambda b, h, i: (0, 0)),
            pl.BlockSpec((1, LANE), lambda b, h, i: (0, 0)),
            q_spec, k_spec, vt_spec,
        ],
        out_specs=o_spec,
        out_shape=jax.ShapeDtypeStruct((N_PAIRS, batch * seq, LANE), F32),
        scratch_shapes=[pltpu.VMEM((2 * pg, 1, t), F32), pltpu.VMEM((2 * pg, LANE + DENOM_ROWS, t), F32),
                        pltpu.VMEM((2, 2 * pg, t, t), F32), pltpu.VMEM((2, 2 * pg, 1, t), F32),
                        pltpu.VMEM((2 * pg, t, LANE), BF16)],
        compiler_params=_params(3),
        name="differential_attention",
    )(lam_params, sub_g, proj, proj, vt)


def _dsa_attn_kernel(bias_ref, q_ref, k_ref, vt_ref, o_ref, m_ref, acc_ref, s_ref, smax_ref, qh_ref, *, t, pg):
    i = pl.program_id(2)
    _split_query_heads(q_ref, qh_ref, pg)
    _reset_softmax_state(m_ref, acc_ref)

    def consume(slot, kb, diagonal):
        del diagonal
        for hh in range(2 * pg):
            h = hh % 2
            _softmax_block(hh, s_ref[slot, hh], smax_ref[slot, hh],
                           vt_ref[hh // 2, kb, h * HEAD_DIM:(h + 1) * HEAD_DIM, :], m_ref, acc_ref)

    def bias_of(kb):
        return bias_ref[0, 0, kb].astype(F32)

    _pipelined_key_loop(i, functools.partial(_score_tiles_into, s_ref, smax_ref, qh_ref, k_ref, t, pg, bias_of),
                        consume)
    for p in range(pg):
        o_ref[p] = jnp.concatenate([_normalized(acc_ref, 2 * p, HEAD_DIM),
                                    _normalized(acc_ref, 2 * p + 1, HEAD_DIM)], axis=0).T


def _dsa_attention(proj, vt, bias, batch, seq, t=T_BLOCK, pg=PAIRS_PER_STEP):
    nq = seq // t
    q_spec, k_spec, vt_spec, o_spec = _attn_specs(nq, seq, t, pg)
    return pl.pallas_call(
        functools.partial(_dsa_attn_kernel, t=t, pg=pg),
        grid=(batch, N_PAIRS // pg, nq),
        in_specs=[
            pl.BlockSpec((1, 1, nq, t, t), lambda b, h, i: (b, i, 0, 0, 0)),
            q_spec, k_spec, vt_spec,
        ],
        out_specs=o_spec,
        out_shape=jax.ShapeDtypeStruct((N_PAIRS, batch * seq, LANE), F32),
        scratch_shapes=[pltpu.VMEM((2 * pg, 1, t), F32), pltpu.VMEM((2 * pg, HEAD_DIM + DENOM_ROWS, t), F32),
                        pltpu.VMEM((2, 2 * pg, t, t), F32), pltpu.VMEM((2, 2 * pg, 1, t), F32),
                        pltpu.VMEM((2 * pg, t, LANE), BF16)],
        compiler_params=_params(3),
        name="dsa_sparse_attention",
    )(bias, proj, proj, vt)


def _dsa_select_kernel(qi_ref, ki_ref, wi_ref, bias_ref, key_ref, thr_ref, hi_ref, nge_ref, cnt_ref, run_ref, qh_ref,
                       wb_ref, lmax_ref, *, t, k_top):
    i = pl.program_id(1)
    nq = bias_ref.shape[2]
    n_groups = t // LANE
    r = lax.broadcasted_iota(jnp.int32, (t, t), 0)
    c = lax.broadcasted_iota(jnp.int32, (t, t), 1)
    adm = (c // CHUNK) <= (r // CHUNK)
    earlier_keys = jnp.where(r < c, 1.0, 0.0).astype(BF16)
    wi = wi_ref[0]
    _split_query_heads(qi_ref, qh_ref, IDX_PAIRS)
    for hh in range(IDX_HEADS):
        wb_ref[hh] = jnp.broadcast_to(wi[:, hh:hh + 1], (t, LANE))

    def score_block(kb, diagonal):
        ks = ki_ref[0, pl.ds(pl.multiple_of(kb * t, t), t), :]
        rels = [lax.dot_general(qh_ref[hh], ks, _NT, preferred_element_type=F32) for hh in range(IDX_HEADS)]
        sc = None
        for hh in range(IDX_HEADS):
            term = jnp.concatenate([jnp.maximum(rels[hh][:, g * LANE:(g + 1) * LANE], 0.0) * wb_ref[hh]
                                    for g in range(n_groups)], axis=1)
            sc = term if sc is None else sc + term
        if diagonal:
            sc = jnp.where(adm, sc, NEG_INF)
        bits = pltpu.bitcast(sc, jnp.int32)
        srt = bits ^ ((bits >> 31) & 0x7FFFFFFF)
        srt = jnp.where(sc == 0.0, 0, srt)
        key_ref[kb] = srt
        for g in range(n_groups):
            lmax_ref[...] = jnp.maximum(lmax_ref[...], srt[:, g * LANE:(g + 1) * LANE])

    def score_body(kb, _):
        score_block(kb, False)
        return 0

    lmax_ref[...] = jnp.full(lmax_ref.shape, INT32_MIN, jnp.int32)
    lax.fori_loop(0, i, score_body, 0)
    score_block(i, True)

    def count_where(pred):
        cnt_ref[...] = jnp.zeros_like(cnt_ref)

        def body(kb, _):
            kk = key_ref[kb]
            part = cnt_ref[...]
            for g in range(n_groups):
                part = part + jnp.where(pred(kk[:, g * LANE:(g + 1) * LANE]), 1.0, 0.0)
            cnt_ref[...] = part
            return 0

        lax.fori_loop(0, i + 1, body, 0)
        return jnp.sum(cnt_ref[...], axis=-1, keepdims=True)

    row_max = jnp.broadcast_to(jnp.max(lmax_ref[...], axis=-1, keepdims=True), thr_ref.shape)
    guess = jnp.where(row_max >= INT32_MIN + BRACKET, row_max - BRACKET, INT32_MIN)
    n_guess = count_where(lambda kk: kk >= guess)
    guess_ok = n_guess >= float(k_top)
    thr_ref[...] = jnp.where(guess_ok, guess, INT32_MIN)
    nge_ref[...] = jnp.where(guess_ok, n_guess, float(t) * (i + 1).astype(F32)) + jnp.zeros(nge_ref.shape, F32)
    hi_ref[...] = row_max + 1

    def bis_body(_, carry):
        lo, hi = thr_ref[...], hi_ref[...]
        mid = lo + lax.shift_right_logical(hi - lo, 1)
        n_ge = count_where(lambda kk: kk >= mid)
        up = n_ge >= float(k_top)
        thr_ref[...] = jnp.where(up, mid, lo)
        hi_ref[...] = jnp.where(up, hi, mid)
        nge_ref[...] = jnp.where(up, n_ge, nge_ref[...])
        return carry

    all_guessed = jnp.min(jnp.where(guess_ok, 1, 0)) == 1
    lax.fori_loop(0, jnp.where(all_guessed, BRACKET_STEPS, 32), bis_body, 0)
    thr = thr_ref[...]

    surplus_ties = jnp.max(nge_ref[...]) > float(k_top)

    def plain_block(kb, diagonal):
        kk = key_ref[kb]
        bias = jnp.concatenate(
            [jnp.where(kk[:, g * LANE:(g + 1) * LANE] >= thr, 0.0, NEG_INF) for g in range(n_groups)], axis=1)
        if diagonal:
            bias = jnp.where(adm, bias, NEG_INF)
        bias_ref[0, 0, kb] = bias.T.astype(BF16)

    @pl.when(jnp.logical_not(surplus_ties))
    def _():
        def plain_body(kb, _):
            plain_block(kb, False)
            return 0

        lax.fori_loop(0, i, plain_body, 0)
        plain_block(i, True)

    def tie_block(kb, diagonal, need):
        kk = key_ref[kb]
        eq = jnp.concatenate(
            [jnp.where(kk[:, g * LANE:(g + 1) * LANE] == thr, 1.0, 0.0) for g in range(n_groups)], axis=1)
        rank = jnp.dot(eq.astype(BF16), earlier_keys, preferred_element_type=F32)
        parts = []
        for g in range(n_groups):
            sl = slice(g * LANE, (g + 1) * LANE)
            tie_ok = (rank[:, sl] + run_ref[...]) < need
            parts.append(jnp.where(kk[:, sl] > thr, 0.0,
                                   jnp.where(kk[:, sl] == thr, jnp.where(tie_ok, 0.0, NEG_INF), NEG_INF)))
        bias = jnp.concatenate(parts, axis=1)
        if diagonal:
            bias = jnp.where(adm, bias, NEG_INF)
        bias_ref[0, 0, kb] = bias.T.astype(BF16)
        run_ref[...] = run_ref[...] + jnp.sum(eq, axis=-1, keepdims=True)

    @pl.when(surplus_ties)
    def _():
        need = float(k_top) - count_where(lambda kk: kk > thr)
        run_ref[...] = jnp.zeros_like(run_ref)

        def tie_body(kb, _):
            tie_block(kb, False, need)
            return 0

        lax.fori_loop(0, i, tie_body, 0)
        tie_block(i, True, need)

    def fill_body(kb, _):
        bias_ref[0, 0, kb] = jnp.full((t, t), NEG_INF, BF16)
        return 0

    lax.fori_loop(i + 1, nq, fill_body, 0)


def _dsa_select(proj, wi, batch, seq, t=T_BLOCK):
    nq = seq // t
    k_top = min(TOPK_MAX, seq // 4)
    return pl.pallas_call(
        functools.partial(_dsa_select_kernel, t=t, k_top=k_top),
        grid=(batch, nq),
        in_specs=[
            pl.BlockSpec((IDX_PAIRS, t, LANE), lambda b, i: (QI_BLOCK0 // IDX_PAIRS, b * nq + i, 0)),
            pl.BlockSpec((1, seq, LANE), lambda b, i: (KI_BLOCK, b, 0)),
            pl.BlockSpec((1, t, LANE), lambda b, i: (0, b * nq + i, 0)),
        ],
        out_specs=pl.BlockSpec((1, 1, nq, t, t), lambda b, i: (b, i, 0, 0, 0)),
        out_shape=jax.ShapeDtypeStruct((batch, nq, nq, t, t), BF16),
        scratch_shapes=[
            pltpu.VMEM((nq, t, t), jnp.int32),
            pltpu.VMEM((t, LANE), jnp.int32),
            pltpu.VMEM((t, LANE), jnp.int32),
            pltpu.VMEM((t, LANE), F32),
            pltpu.VMEM((t, LANE), F32),
            pltpu.VMEM((t, LANE), F32),
            pltpu.VMEM((IDX_HEADS, t, LANE), BF16),
            pltpu.VMEM((IDX_HEADS, t, LANE), F32),
            pltpu.VMEM((t, LANE), jnp.int32),
        ],
        compiler_params=_params(2),
        name="dsa_indexer_select",
    )(proj, proj, wi)


def _rope_tables(seq):
    inv = 1.0 / (ROPE_THETA ** (jnp.arange(0, HEAD_DIM, 2, dtype=F32) / HEAD_DIM))
    ang = jnp.arange(seq, dtype=F32)[:, None] * inv[None, :]
    cos, sin = jnp.cos(ang), jnp.sin(ang)
    zero = jnp.zeros_like(sin)
    n_heads = LANE // HEAD_DIM
    cos_t = jnp.tile(cos, (1, 2 * n_heads))
    sin_first = jnp.tile(jnp.concatenate([-sin, zero], axis=1), (1, n_heads))
    sin_second = jnp.tile(jnp.concatenate([zero, sin], axis=1), (1, n_heads))
    return cos_t, sin_first, sin_second


def _stream_ops(q_rope, k_rope, q_scale):
    q_ops = tuple((q_rope, q_scale, ("main", Q_BLOCK0 + p)) for p in range(N_PAIRS))
    k_ops = tuple((k_rope, 1.0, ("main", K_BLOCK0 + p)) for p in range(N_PAIRS))
    v_ops = tuple((False, 1.0, ("vt", p)) for p in range(N_PAIRS))
    g_ops = tuple((False, 1.0, ("main", G_BLOCK0 + p)) for p in range(N_PAIRS))
    return q_ops + k_ops + v_ops + g_ops


def _one_layer(layer, x2d, params, rope_tabs, batch, seq):
    d = x2d.shape[1]
    mixer, j = layer % N_MIXERS, layer // N_MIXERS
    main = 4 * INNER
    if mixer == 0:
        w = params["w_in_a"][j]
        w_idx_q = w[:, main:main + IDX_HEADS * IDX_DIM]
        w_idx_w = w[:, main + IDX_HEADS * IDX_DIM:main + IDX_HEADS * IDX_DIM + IDX_HEADS]
        w_idx_k = w[:, main + IDX_HEADS * IDX_DIM + IDX_HEADS:]
        w_pad = jnp.concatenate(
            [w[:, :main], w_idx_q, w_idx_k, w_idx_k, w_idx_w, jnp.zeros((d, LANE - IDX_HEADS), w.dtype)],
            axis=1).astype(BF16)
        ops = (_stream_ops(True, True, LOGIT_SCALE * LOG2E)
               + tuple((True, IDX_DIM ** -0.5, ("main", QI_BLOCK0 + p)) for p in range(IDX_PAIRS))
               + ((True, 1.0, ("main", KI_BLOCK)), (False, IDX_HEADS ** -0.5, ("f32", 0))))
        proj, vt, wi = _in_proj(x2d, w_pad, rope_tabs, ops, seq)
        bias = _dsa_select(proj, wi, batch, seq)
        o = _dsa_attention(proj, vt, bias, batch, seq)
        w_out = params["w_out_a"][j]
    elif mixer == 1:
        proj, vt, _ = _in_proj(x2d, params["w_in_b"][j].astype(BF16), rope_tabs, _stream_ops(False, False, LOGIT_SCALE), seq)
        o = _sb_attention(proj, vt, batch, seq)
        w_out = params["w_out_b"][j]
    else:
        lambda_init = 0.8 - 0.6 * math.exp(-0.3 * layer)
        proj, vt, _ = _in_proj(x2d, params["w_in_c"][j].astype(BF16), rope_tabs, _stream_ops(True, True, LOGIT_SCALE * LOG2E),
                               seq)
        lam_rows = jnp.stack([params["lambda_q1"][j], params["lambda_k1"][j],
                              params["lambda_q2"][j], params["lambda_k2"][j]]).astype(F32)
        lam_params = jnp.pad(lam_rows, ((0, 0), (0, LANE - DIFF_DIM)))
        sub_g = params["subln_g"][j].reshape(1, 2 * DIFF_DIM).astype(F32)
        o = _diff_attention(proj, vt, lam_params, sub_g, lambda_init, batch, seq)
        w_out = params["w_out_c"][j]
    return _out_proj_ln(o, proj, x2d, w_out.astype(BF16), params["ln_g"][layer], params["ln_b"][layer])


def kernel(x, w_in_a, w_out_a, w_in_b, w_out_b, w_in_c, w_out_c,
           lambda_q1, lambda_k1, lambda_q2, lambda_k2, subln_g, ln_g, ln_b):
    batch, seq, d = x.shape
    params = dict(w_in_a=w_in_a, w_out_a=w_out_a, w_in_b=w_in_b, w_out_b=w_out_b, w_in_c=w_in_c, w_out_c=w_out_c,
                  lambda_q1=lambda_q1, lambda_k1=lambda_k1, lambda_q2=lambda_q2, lambda_k2=lambda_k2,
                  subln_g=subln_g, ln_g=ln_g, ln_b=ln_b)
    x2d = x.reshape(batch * seq, d)
    rope_tabs = _rope_tables(seq)
    for layer in range(DEPTH):
        x2d = _one_layer(layer, x2d, params, rope_tabs, batch, seq)
    return x2d.reshape(batch, seq, d)
```

```python
import functools
import math

import jax
import jax.numpy as jnp
from jax import lax
from jax.experimental import pallas as pl
from jax.experimental.pallas import tpu as pltpu

F32 = jnp.float32
BF16 = jnp.bfloat16

LANE = 128
V7X_VMEM_LIMIT_BYTES = 56 * 1024 * 1024

D_MODEL = 1024
DEPTH = 4
CHUNK = 64
N_MIXERS = 3
N_HEADS = 16
HEAD_DIM = 64
INNER = N_HEADS * HEAD_DIM
N_PAIRS = INNER // LANE
IDX_HEADS = 8
IDX_DIM = 64
IDX_PAIRS = IDX_HEADS * IDX_DIM // LANE
TOPK_MAX = 256
DIFF_DIM = HEAD_DIM
ROPE_THETA = 10000.0
LN_EPS = 1e-5
RMS_EPS = 1e-5
ALPHA = (2.0 * DEPTH) ** 0.25
LOGIT_SCALE = HEAD_DIM ** -0.5
NEG_INF = float("-inf")
INT32_MIN = -2 ** 31
BRACKET = 4 << 23
BRACKET_STEPS = 26
LOG2E = 1.4426950408889634
EXP_UNDERFLOW = -104.5

DENOM_ROWS = 16
T_BLOCK = 256
PAIRS_PER_STEP = 4
SCORE_LEAD = 2

Q_BLOCK0, K_BLOCK0, G_BLOCK0, QI_BLOCK0 = 0, N_PAIRS, 2 * N_PAIRS, 3 * N_PAIRS
KI_BLOCK = QI_BLOCK0 + IDX_PAIRS

_NT = (((1,), (1,)), ((), ()))


def _params(n_grid_dims):
    return pltpu.CompilerParams(
        dimension_semantics=("arbitrary",) * n_grid_dims,
        vmem_limit_bytes=V7X_VMEM_LIMIT_BYTES,
    )


def _in_proj_kernel(x_ref, w_ref, cos_ref, sa_ref, sb_ref, main_ref, vt_ref, *f32_ref, ops, chunk, t):
    x = x_ref[...].astype(BF16)
    n_blocks = len(ops)
    tm = x.shape[0]
    for c0 in range(0, n_blocks, chunk):
        nb = min(chunk, n_blocks - c0)
        acc = jnp.dot(x, w_ref[:, c0 * LANE:(c0 + nb) * LANE], preferred_element_type=F32)
        for j in range(nb):
            rope, scale, (dest, slot) = ops[c0 + j]
            y = acc[:, j * LANE:(j + 1) * LANE]
            if rope:
                y = (y * cos_ref[...] + pltpu.roll(y, LANE - 32, 1) * sa_ref[...]
                     + pltpu.roll(y, 32, 1) * sb_ref[...])
            if scale != 1.0:
                y = y * scale
            if dest == "main":
                main_ref[slot] = y.astype(BF16)
            elif dest == "vt":
                y_t = y.T
                for kb in range(tm // t):
                    vt_ref[slot, kb] = y_t[:, kb * t:(kb + 1) * t].astype(BF16)
            else:
                f32_ref[0][slot] = y


def _in_proj(x2d, w, rope_tabs, ops, seq, tm=512, chunk=4):
    m, d = x2d.shape
    t = T_BLOCK
    assert w.shape == (d, len(ops) * LANE) and m % tm == 0 and seq % tm == 0 and tm % t == 0
    n_main = sum(1 for o in ops if o[2][0] == "main")
    n_f32 = sum(1 for o in ops if o[2][0] == "f32")
    s_tiles = seq // tm
    out_shape = [jax.ShapeDtypeStruct((n_main, m, LANE), BF16),
                 jax.ShapeDtypeStruct((N_PAIRS, m // t, LANE, t), BF16)]
    out_specs = [pl.BlockSpec((n_main, tm, LANE), lambda i: (0, i, 0)),
                 pl.BlockSpec((N_PAIRS, tm // t, LANE, t), lambda i: (0, i, 0, 0))]
    if n_f32:
        out_shape.append(jax.ShapeDtypeStruct((n_f32, m, LANE), F32))
        out_specs.append(pl.BlockSpec((n_f32, tm, LANE), lambda i: (0, i, 0)))
    tab_spec = pl.BlockSpec((tm, LANE), lambda i: (i % s_tiles, 0))
    res = pl.pallas_call(
        functools.partial(_in_proj_kernel, ops=ops, chunk=chunk, t=t),
        grid=(m // tm,),
        in_specs=[
            pl.BlockSpec((tm, d), lambda i: (i, 0)),
            pl.BlockSpec((d, len(ops) * LANE), lambda i: (0, 0)),
            tab_spec, tab_spec, tab_spec,
        ],
        out_specs=out_specs,
        out_shape=out_shape,
        compiler_params=_params(1),
        name="in_proj",
    )(x2d, w, *rope_tabs)
    return res if n_f32 else (res[0], res[1], None)


def _out_proj_ln_kernel(o_ref, g_ref, x_ref, w_ref, lng_ref, lnb_ref, out_ref):
    parts = []
    for p in range(N_PAIRS):
        g = g_ref[p].astype(F32)
        gate = g * (1.0 / (1.0 + jnp.exp(-g)))
        parts.append((o_ref[p] * gate).astype(BF16))
    og = jnp.concatenate(parts, axis=1)
    y = jnp.dot(og, w_ref[...], preferred_element_type=F32)
    z = ALPHA * x_ref[...] + y
    mu = jnp.mean(z, axis=-1, keepdims=True)
    zc = z - mu
    var = jnp.mean(zc * zc, axis=-1, keepdims=True)
    out_ref[...] = zc * lax.rsqrt(var + LN_EPS) * lng_ref[...] + lnb_ref[...]


def _out_proj_ln(o, proj, x2d, w_out, ln_g, ln_b, tm=512):
    m, d = x2d.shape
    return pl.pallas_call(
        _out_proj_ln_kernel,
        grid=(m // tm,),
        in_specs=[
            pl.BlockSpec((N_PAIRS, tm, LANE), lambda i: (0, i, 0)),
            pl.BlockSpec((N_PAIRS, tm, LANE), lambda i: (G_BLOCK0 // N_PAIRS, i, 0)),
            pl.BlockSpec((tm, d), lambda i: (i, 0)),
            pl.BlockSpec((INNER, d), lambda i: (0, 0)),
            pl.BlockSpec((1, d), lambda i: (0, 0)),
            pl.BlockSpec((1, d), lambda i: (0, 0)),
        ],
        out_specs=pl.BlockSpec((tm, d), lambda i: (i, 0)),
        out_shape=jax.ShapeDtypeStruct((m, d), F32),
        compiler_params=_params(1),
        name="out_proj_ln",
    )(o, proj, x2d, w_out, ln_g.reshape(1, d), ln_b.reshape(1, d))


def _head_query(q_pair, h):
    is_a = lax.broadcasted_iota(jnp.int32, (1, LANE), 1) < HEAD_DIM
    keep = is_a if h == 0 else jnp.logical_not(is_a)
    return jnp.where(keep, q_pair, jnp.zeros_like(q_pair))


def _key_query_iota(t):
    key = lax.broadcasted_iota(jnp.int32, (t, t), 0)
    qry = lax.broadcasted_iota(jnp.int32, (t, t), 1)
    return key, qry


def _attn_specs(n_q_blocks, seq, t, pg):
    q_spec = pl.BlockSpec((pg, t, LANE), lambda b, h, i: (Q_BLOCK0 // pg + h, b * n_q_blocks + i, 0))
    k_spec = pl.BlockSpec((pg, seq, LANE), lambda b, h, i: (K_BLOCK0 // pg + h, b, 0))
    vt_spec = pl.BlockSpec((pg, seq // t, LANE, t), lambda b, h, i: (h, b, 0, 0))
    o_spec = pl.BlockSpec((pg, t, LANE), lambda b, h, i: (h, b * n_q_blocks + i, 0))
    return q_spec, k_spec, vt_spec, o_spec


def _key_block(k_ref, p, kb, t):
    return k_ref[p, pl.ds(pl.multiple_of(kb * t, t), t), :]


def _split_query_heads(q_ref, qh_ref, n_pairs):
    for p in range(n_pairs):
        for h in range(2):
            qh_ref[2 * p + h] = _head_query(q_ref[p], h)


def _sb_kernel(q_ref, k_ref, vt_ref, o_ref, acc_ref, carry_ref, qh_ref, *, t, pg):
    i = pl.program_id(2)
    _split_query_heads(q_ref, qh_ref, pg)
    key, qry = _key_query_iota(t)
    before = key < qry
    later_keys = jnp.where(qry > key, 1.0, 0.0).astype(BF16)
    acc_ref[...] = jnp.zeros_like(acc_ref)
    carry_ref[...] = jnp.zeros_like(carry_ref)

    def block_step(kb, diagonal):
        zs = [lax.dot_general(_key_block(k_ref, hh // 2, kb, t), qh_ref[hh], _NT, preferred_element_type=F32)
              for hh in range(2 * pg)]
        pre, cum = [], []
        for hh in range(2 * pg):
            z = zs[hh]
            soft = jnp.log(1.0 + jnp.exp2(jnp.abs(z) * -LOG2E))
            log_sig = jnp.minimum(z, 0.0) - soft
            log_om = log_sig - z
            if diagonal:
                log_om = jnp.where(before, log_om, 0.0)
            hi = log_om.astype(BF16)
            lo = (log_om - hi.astype(F32)).astype(BF16)
            c = (jnp.dot(later_keys, hi, preferred_element_type=F32)
                 + jnp.dot(later_keys, lo, preferred_element_type=F32))
            cum.append(c)
            carry = carry_ref[hh:hh + 1, :]
            pre.append(log_sig + carry)
            carry_ref[hh:hh + 1, :] = carry + c[0:1, :] + log_om[0:1, :]
        for hh in range(2 * pg):
            a = jnp.exp(pre[hh] + cum[hh])
            if diagonal:
                a = jnp.where(before, a, 0.0)
            h = hh % 2
            acc_ref[hh] += jnp.dot(vt_ref[hh // 2, kb, h * HEAD_DIM:(h + 1) * HEAD_DIM, :], a.astype(BF16),
                                   preferred_element_type=F32)

    def some_weight_left():
        return (jnp.max(carry_ref[...]) >= EXP_UNDERFLOW).astype(jnp.int32)

    block_step(i, True)

    def kb_cond(state):
        j, go = state
        return jnp.logical_and(j < i, go == 1)

    def kb_body(state):
        j, _ = state
        block_step(i - 1 - j, False)
        return j + 1, some_weight_left()

    lax.while_loop(kb_cond, kb_body, (jnp.int32(0), some_weight_left()))
    for p in range(pg):
        o_ref[p] = jnp.concatenate([acc_ref[2 * p], acc_ref[2 * p + 1]], axis=0).T


def _sb_attention(proj, vt, batch, seq, t=T_BLOCK, pg=PAIRS_PER_STEP):
    nq = seq // t
    q_spec, k_spec, vt_spec, o_spec = _attn_specs(nq, seq, t, pg)
    return pl.pallas_call(
        functools.partial(_sb_kernel, t=t, pg=pg),
        grid=(batch, N_PAIRS // pg, nq),
        in_specs=[q_spec, k_spec, vt_spec],
        out_specs=o_spec,
        out_shape=jax.ShapeDtypeStruct((N_PAIRS, batch * seq, LANE), F32),
        scratch_shapes=[pltpu.VMEM((2 * pg, HEAD_DIM, t), F32), pltpu.VMEM((2 * pg, t), F32),
                        pltpu.VMEM((2 * pg, t, LANE), BF16)],
        compiler_params=_params(3),
        name="stick_breaking_attention",
    )(proj, proj, vt)


def _softmax_block(hh, s, s_max, vt_rows, m_ref, acc_ref):
    m_prev = m_ref[hh]
    m_new = jnp.maximum(m_prev, s_max)
    m_safe = jnp.where(m_new == NEG_INF, 0.0, m_new)
    alpha = jnp.exp2(m_prev - m_safe)
    p = jnp.exp2(s - m_safe).astype(BF16)
    lhs = jnp.concatenate([vt_rows, jnp.ones((DENOM_ROWS, vt_rows.shape[1]), BF16)], axis=0)
    acc_ref[hh] = alpha * acc_ref[hh] + jnp.dot(lhs, p, preferred_element_type=F32)
    m_ref[hh] = m_new


def _reset_softmax_state(m_ref, acc_ref):
    m_ref[...] = jnp.full(m_ref.shape, NEG_INF, F32)
    acc_ref[...] = jnp.zeros_like(acc_ref)


def _normalized(acc_ref, hh, n_rows):
    acc = acc_ref[hh]
    return acc[:n_rows] / acc[n_rows:n_rows + 1]


def _score_tile_into(s_ref, smax_ref, qh_ref, k_ref, t, slot, kb, hh, bias):
    s = lax.dot_general(_key_block(k_ref, hh // 2, kb, t), qh_ref[hh], _NT, preferred_element_type=F32)
    if bias is not None:
        s = s + bias
    s_ref[slot, hh] = s
    smax_ref[slot, hh] = jnp.max(s, axis=0, keepdims=True)


def _pipelined_key_loop(i, n_heads, score_into, consume, bias_of=None):
    def step(next_slot, next_kb, cur_slot, cur_kb, diagonal):
        bias = None if (bias_of is None or next_slot is None) else bias_of(next_kb)
        for h in range(n_heads + SCORE_LEAD):
            if next_slot is not None and h < n_heads:
                score_into(next_slot, next_kb, h, bias)
            if cur_slot is not None and h >= SCORE_LEAD:
                consume(cur_slot, cur_kb, h - SCORE_LEAD, diagonal)

    step(0, 0, None, None, False)

    def pair_body(jj, _):
        kb = 2 * jj
        step(1, kb + 1, 0, kb, False)
        step(0, kb + 2, 1, kb + 1, False)
        return 0

    lax.fori_loop(0, i // 2, pair_body, 0)

    @pl.when(i % 2 == 1)
    def _():
        step(1, i, 0, i - 1, False)

    step(None, None, i % 2, i, True)


def _diff_kernel(lam_ref, subg_ref, q_ref, k_ref, vt_ref, o_ref, m_ref, acc_ref, s_ref, smax_ref, qh_ref, *, t, pg,
                 lambda_init):
    i = pl.program_id(2)
    _split_query_heads(q_ref, qh_ref, pg)
    key, qry = _key_query_iota(t)
    adm = (key // CHUNK) <= (qry // CHUNK)
    _reset_softmax_state(m_ref, acc_ref)

    def consume(slot, kb, hh, diagonal):
        s = s_ref[slot, hh]
        if diagonal:
            s = jnp.where(adm, s, NEG_INF)
            s_max = jnp.max(s, axis=0, keepdims=True)
        else:
            s_max = smax_ref[slot, hh]
        _softmax_block(hh, s, s_max, vt_ref[hh // 2, kb], m_ref, acc_ref)

    _pipelined_key_loop(i, 2 * pg, functools.partial(_score_tile_into, s_ref, smax_ref, qh_ref, k_ref, t), consume)

    lp = lam_ref[...]
    lam = (jnp.exp(jnp.sum(lp[0:1] * lp[1:2], axis=-1, keepdims=True))
           - jnp.exp(jnp.sum(lp[2:3] * lp[3:4], axis=-1, keepdims=True)) + lambda_init)
    for p in range(pg):
        d = _normalized(acc_ref, 2 * p, LANE) - lam * _normalized(acc_ref, 2 * p + 1, LANE)
        ms = jnp.mean(d * d, axis=0, keepdims=True)
        o_ref[p] = (d * lax.rsqrt(ms + RMS_EPS)).T * subg_ref[...] * (1.0 - lambda_init)


def _diff_attention(proj, vt, lam_params, sub_g, lambda_init, batch, seq, t=T_BLOCK, pg=PAIRS_PER_STEP):
    nq = seq // t
    q_spec, k_spec, vt_spec, o_spec = _attn_specs(nq, seq, t, pg)
    return pl.pallas_call(
        functools.partial(_diff_kernel, t=t, pg=pg, lambda_init=lambda_init),
        grid=(batch, N_PAIRS // pg, nq),
        in_specs=[
            pl.BlockSpec((4, LANE), lambda b, h, i: (0, 0)),
            pl.BlockSpec((1, LANE), lambda b, h, i: (0, 0)),
            q_spec, k_spec, vt_spec,
        ],
        out_specs=o_spec,
        out_shape=jax.ShapeDtypeStruct((N_PAIRS, batch * seq, LANE), F32),
        scratch_shapes=[pltpu.VMEM((2 * pg, 1, t), F32), pltpu.VMEM((2 * pg, LANE + DENOM_ROWS, t), F32),
                        pltpu.VMEM((2, 2 * pg, t, t), F32), pltpu.VMEM((2, 2 * pg, 1, t), F32),
                        pltpu.VMEM((2 * pg, t, LANE), BF16)],
        compiler_params=_params(3),
        name="differential_attention",
    )(lam_params, sub_g, proj, proj, vt)


def _dsa_attn_kernel(bias_ref, q_ref, k_ref, vt_ref, o_ref, m_ref, acc_ref, s_ref, smax_ref, qh_ref, *, t, pg):
    i = pl.program_id(2)
    _split_query_heads(q_ref, qh_ref, pg)
    _reset_softmax_state(m_ref, acc_ref)

    def consume(slot, kb, hh, diagonal):
        del diagonal
        h = hh % 2
        _softmax_block(hh, s_ref[slot, hh], smax_ref[slot, hh],
                       vt_ref[hh // 2, kb, h * HEAD_DIM:(h + 1) * HEAD_DIM, :], m_ref, acc_ref)

    def bias_of(kb):
        return bias_ref[0, 0, kb].astype(F32)

    _pipelined_key_loop(i, 2 * pg, functools.partial(_score_tile_into, s_ref, smax_ref, qh_ref, k_ref, t), consume,
                        bias_of)
    for p in range(pg):
        o_ref[p] = jnp.concatenate([_normalized(acc_ref, 2 * p, HEAD_DIM),
                                    _normalized(acc_ref, 2 * p + 1, HEAD_DIM)], axis=0).T


def _dsa_attention(proj, vt, bias, batch, seq, t=T_BLOCK, pg=PAIRS_PER_STEP):
    nq = seq // t
    q_spec, k_spec, vt_spec, o_spec = _attn_specs(nq, seq, t, pg)
    return pl.pallas_call(
        functools.partial(_dsa_attn_kernel, t=t, pg=pg),
        grid=(batch, N_PAIRS // pg, nq),
        in_specs=[
            pl.BlockSpec((1, 1, nq, t, t), lambda b, h, i: (b, i, 0, 0, 0)),
            q_spec, k_spec, vt_spec,
        ],
        out_specs=o_spec,
        out_shape=jax.ShapeDtypeStruct((N_PAIRS, batch * seq, LANE), F32),
        scratch_shapes=[pltpu.VMEM((2 * pg, 1, t), F32), pltpu.VMEM((2 * pg, HEAD_DIM + DENOM_ROWS, t), F32),
                        pltpu.VMEM((2, 2 * pg, t, t), F32), pltpu.VMEM((2, 2 * pg, 1, t), F32),
                        pltpu.VMEM((2 * pg, t, LANE), BF16)],
        compiler_params=_params(3),
        name="dsa_sparse_attention",
    )(bias, proj, proj, vt)


def _dsa_select_kernel(qi_ref, ki_ref, wi_ref, bias_ref, key_ref, thr_ref, hi_ref, nge_ref, cnt_ref, run_ref, qh_ref,
                       wb_ref, lmax_ref, *, t, k_top):
    i = pl.program_id(1)
    nq = bias_ref.shape[2]
    n_groups = t // LANE
    r = lax.broadcasted_iota(jnp.int32, (t, t), 0)
    c = lax.broadcasted_iota(jnp.int32, (t, t), 1)
    adm = (c // CHUNK) <= (r // CHUNK)
    earlier_keys = jnp.where(r < c, 1.0, 0.0).astype(BF16)
    wi = wi_ref[0]
    _split_query_heads(qi_ref, qh_ref, IDX_PAIRS)
    for hh in range(IDX_HEADS):
        wb_ref[hh] = jnp.broadcast_to(wi[:, hh:hh + 1], (t, LANE))

    def score_block(kb, diagonal):
        ks = ki_ref[0, pl.ds(pl.multiple_of(kb * t, t), t), :]
        rels = [lax.dot_general(qh_ref[hh], ks, _NT, preferred_element_type=F32) for hh in range(IDX_HEADS)]
        sc = None
        for hh in range(IDX_HEADS):
            term = jnp.concatenate([jnp.maximum(rels[hh][:, g * LANE:(g + 1) * LANE], 0.0) * wb_ref[hh]
                                    for g in range(n_groups)], axis=1)
            sc = term if sc is None else sc + term
        if diagonal:
            sc = jnp.where(adm, sc, NEG_INF)
        bits = pltpu.bitcast(sc, jnp.int32)
        srt = bits ^ ((bits >> 31) & 0x7FFFFFFF)
        srt = jnp.where(sc == 0.0, 0, srt)
        key_ref[kb] = srt
        for g in range(n_groups):
            lmax_ref[...] = jnp.maximum(lmax_ref[...], srt[:, g * LANE:(g + 1) * LANE])

    def score_body(kb, _):
        score_block(kb, False)
        return 0

    lmax_ref[...] = jnp.full(lmax_ref.shape, INT32_MIN, jnp.int32)
    lax.fori_loop(0, i, score_body, 0)
    score_block(i, True)

    def count_where(pred):
        cnt_ref[...] = jnp.zeros_like(cnt_ref)

        def body(kb, _):
            kk = key_ref[kb]
            part = cnt_ref[...]
            for g in range(n_groups):
                part = part + jnp.where(pred(kk[:, g * LANE:(g + 1) * LANE]), 1.0, 0.0)
            cnt_ref[...] = part
            return 0

        lax.fori_loop(0, i + 1, body, 0)
        return jnp.sum(cnt_ref[...], axis=-1, keepdims=True)

    row_max = jnp.broadcast_to(jnp.max(lmax_ref[...], axis=-1, keepdims=True), thr_ref.shape)
    guess = jnp.where(row_max >= INT32_MIN + BRACKET, row_max - BRACKET, INT32_MIN)
    n_guess = count_where(lambda kk: kk >= guess)
    guess_ok = n_guess >= float(k_top)
    thr_ref[...] = jnp.where(guess_ok, guess, INT32_MIN)
    nge_ref[...] = jnp.where(guess_ok, n_guess, float(t) * (i + 1).astype(F32)) + jnp.zeros(nge_ref.shape, F32)
    hi_ref[...] = row_max + 1

    def bis_body(_, carry):
        lo, hi = thr_ref[...], hi_ref[...]
        mid = lo + lax.shift_right_logical(hi - lo, 1)
        n_ge = count_where(lambda kk: kk >= mid)
        up = n_ge >= float(k_top)
        thr_ref[...] = jnp.where(up, mid, lo)
        hi_ref[...] = jnp.where(up, hi, mid)
        nge_ref[...] = jnp.where(up, n_ge, nge_ref[...])
        return carry

    all_guessed = jnp.min(jnp.where(guess_ok, 1, 0)) == 1
    lax.fori_loop(0, jnp.where(all_guessed, BRACKET_STEPS, 32), bis_body, 0)
    thr = thr_ref[...]

    surplus_ties = jnp.max(nge_ref[...]) > float(k_top)

    def plain_block(kb, diagonal):
        kk = key_ref[kb]
        bias = jnp.concatenate(
            [jnp.where(kk[:, g * LANE:(g + 1) * LANE] >= thr, 0.0, NEG_INF) for g in range(n_groups)], axis=1)
        if diagonal:
            bias = jnp.where(adm, bias, NEG_INF)
        bias_ref[0, 0, kb] = bias.T.astype(BF16)

    @pl.when(jnp.logical_not(surplus_ties))
    def _():
        def plain_body(kb, _):
            plain_block(kb, False)
            return 0

        lax.fori_loop(0, i, plain_body, 0)
        plain_block(i, True)

    def tie_block(kb, diagonal, need):
        kk = key_ref[kb]
        eq = jnp.concatenate(
            [jnp.where(kk[:, g * LANE:(g + 1) * LANE] == thr, 1.0, 0.0) for g in range(n_groups)], axis=1)
        rank = jnp.dot(eq.astype(BF16), earlier_keys, preferred_element_type=F32)
        parts = []
        for g in range(n_groups):
            sl = slice(g * LANE, (g + 1) * LANE)
            tie_ok = (rank[:, sl] + run_ref[...]) < need
            parts.append(jnp.where(kk[:, sl] > thr, 0.0,
                                   jnp.where(kk[:, sl] == thr, jnp.where(tie_ok, 0.0, NEG_INF), NEG_INF)))
        bias = jnp.concatenate(parts, axis=1)
        if diagonal:
            bias = jnp.where(adm, bias, NEG_INF)
        bias_ref[0, 0, kb] = bias.T.astype(BF16)
        run_ref[...] = run_ref[...] + jnp.sum(eq, axis=-1, keepdims=True)

    @pl.when(surplus_ties)
    def _():
        need = float(k_top) - count_where(lambda kk: kk > thr)
        run_ref[...] = jnp.zeros_like(run_ref)

        def tie_body(kb, _):
            tie_block(kb, False, need)
            return 0

        lax.fori_loop(0, i, tie_body, 0)
        tie_block(i, True, need)

    def fill_body(kb, _):
        bias_ref[0, 0, kb] = jnp.full((t, t), NEG_INF, BF16)
        return 0

    lax.fori_loop(i + 1, nq, fill_body, 0)


def _dsa_select(proj, wi, batch, seq, t=T_BLOCK):
    nq = seq // t
    k_top = min(TOPK_MAX, seq // 4)
    return pl.pallas_call(
        functools.partial(_dsa_select_kernel, t=t, k_top=k_top),
        grid=(batch, nq),
        in_specs=[
            pl.BlockSpec((IDX_PAIRS, t, LANE), lambda b, i: (QI_BLOCK0 // IDX_PAIRS, b * nq + i, 0)),
            pl.BlockSpec((1, seq, LANE), lambda b, i: (KI_BLOCK, b, 0)),
            pl.BlockSpec((1, t, LANE), lambda b, i: (0, b * nq + i, 0)),
        ],
        out_specs=pl.BlockSpec((1, 1, nq, t, t), lambda b, i: (b, i, 0, 0, 0)),
        out_shape=jax.ShapeDtypeStruct((batch, nq, nq, t, t), BF16),
        scratch_shapes=[
            pltpu.VMEM((nq, t, t), jnp.int32),
            pltpu.VMEM((t, LANE), jnp.int32),
            pltpu.VMEM((t, LANE), jnp.int32),
            pltpu.VMEM((t, LANE), F32),
            pltpu.VMEM((t, LANE), F32),
            pltpu.VMEM((t, LANE), F32),
            pltpu.VMEM((IDX_HEADS, t, LANE), BF16),
            pltpu.VMEM((IDX_HEADS, t, LANE), F32),
            pltpu.VMEM((t, LANE), jnp.int32),
        ],
        compiler_params=_params(2),
        name="dsa_indexer_select",
    )(proj, proj, wi)


def _rope_tables(seq):
    inv = 1.0 / (ROPE_THETA ** (jnp.arange(0, HEAD_DIM, 2, dtype=F32) / HEAD_DIM))
    ang = jnp.arange(seq, dtype=F32)[:, None] * inv[None, :]
    cos, sin = jnp.cos(ang), jnp.sin(ang)
    zero = jnp.zeros_like(sin)
    n_heads = LANE // HEAD_DIM
    cos_t = jnp.tile(cos, (1, 2 * n_heads))
    sin_first = jnp.tile(jnp.concatenate([-sin, zero], axis=1), (1, n_heads))
    sin_second = jnp.tile(jnp.concatenate([zero, sin], axis=1), (1, n_heads))
    return cos_t, sin_first, sin_second


def _stream_ops(q_rope, k_rope, q_scale):
    q_ops = tuple((q_rope, q_scale, ("main", Q_BLOCK0 + p)) for p in range(N_PAIRS))
    k_ops = tuple((k_rope, 1.0, ("main", K_BLOCK0 + p)) for p in range(N_PAIRS))
    v_ops = tuple((False, 1.0, ("vt", p)) for p in range(N_PAIRS))
    g_ops = tuple((False, 1.0, ("main", G_BLOCK0 + p)) for p in range(N_PAIRS))
    return q_ops + k_ops + v_ops + g_ops


def _one_layer(layer, x2d, params, rope_tabs, batch, seq):
    d = x2d.shape[1]
    mixer, j = layer % N_MIXERS, layer // N_MIXERS
    main = 4 * INNER
    if mixer == 0:
        w = params["w_in_a"][j]
        w_idx_q = w[:, main:main + IDX_HEADS * IDX_DIM]
        w_idx_w = w[:, main + IDX_HEADS * IDX_DIM:main + IDX_HEADS * IDX_DIM + IDX_HEADS]
        w_idx_k = w[:, main + IDX_HEADS * IDX_DIM + IDX_HEADS:]
        w_pad = jnp.concatenate(
            [w[:, :main], w_idx_q, w_idx_k, w_idx_k, w_idx_w, jnp.zeros((d, LANE - IDX_HEADS), w.dtype)],
            axis=1).astype(BF16)
        ops = (_stream_ops(True, True, LOGIT_SCALE * LOG2E)
               + tuple((True, IDX_DIM ** -0.5, ("main", QI_BLOCK0 + p)) for p in range(IDX_PAIRS))
               + ((True, 1.0, ("main", KI_BLOCK)), (False, IDX_HEADS ** -0.5, ("f32", 0))))
        proj, vt, wi = _in_proj(x2d, w_pad, rope_tabs, ops, seq)
        bias = _dsa_select(proj, wi, batch, seq)
        o = _dsa_attention(proj, vt, bias, batch, seq)
        w_out = params["w_out_a"][j]
    elif mixer == 1:
        proj, vt, _ = _in_proj(x2d, params["w_in_b"][j].astype(BF16), rope_tabs, _stream_ops(False, False, LOGIT_SCALE), seq)
        o = _sb_attention(proj, vt, batch, seq)
        w_out = params["w_out_b"][j]
    else:
        lambda_init = 0.8 - 0.6 * math.exp(-0.3 * layer)
        proj, vt, _ = _in_proj(x2d, params["w_in_c"][j].astype(BF16), rope_tabs, _stream_ops(True, True, LOGIT_SCALE * LOG2E),
                               seq)
        lam_rows = jnp.stack([params["lambda_q1"][j], params["lambda_k1"][j],
                              params["lambda_q2"][j], params["lambda_k2"][j]]).astype(F32)
        lam_params = jnp.pad(lam_rows, ((0, 0), (0, LANE - DIFF_DIM)))
        sub_g = params["subln_g"][j].reshape(1, 2 * DIFF_DIM).astype(F32)
        o = _diff_attention(proj, vt, lam_params, sub_g, lambda_init, batch, seq)
        w_out = params["w_out_c"][j]
    return _out_proj_ln(o, proj, x2d, w_out.astype(BF16), params["ln_g"][layer], params["ln_b"][layer])


def kernel(x, w_in_a, w_out_a, w_in_b, w_out_b, w_in_c, w_out_c,
           lambda_q1, lambda_k1, lambda_q2, lambda_k2, subln_g, ln_g, ln_b):
    batch, seq, d = x.shape
    params = dict(w_in_a=w_in_a, w_out_a=w_out_a, w_in_b=w_in_b, w_out_b=w_out_b, w_in_c=w_in_c, w_out_c=w_out_c,
                  lambda_q1=lambda_q1, lambda_k1=lambda_k1, lambda_q2=lambda_q2, lambda_k2=lambda_k2,
                  subln_g=subln_g, ln_g=ln_g, ln_b=ln_b)
    x2d = x.reshape(batch * seq, d)
    rope_tabs = _rope_tables(seq)
    for layer in range(DEPTH):
        x2d = _one_layer(layer, x2d, params, rope_tabs, batch, seq)
    return x2d.reshape(batch, seq, d)
```

```python
import functools
import math

import jax
import jax.numpy as jnp
from jax import lax
from jax.experimental import pallas as pl
from jax.experimental.pallas import tpu as pltpu

F32 = jnp.float32
BF16 = jnp.bfloat16

LANE = 128
V7X_VMEM_LIMIT_BYTES = 56 * 1024 * 1024

D_MODEL = 1024
DEPTH = 4
CHUNK = 64
N_MIXERS = 3
N_HEADS = 16
HEAD_DIM = 64
INNER = N_HEADS * HEAD_DIM
N_PAIRS = INNER // LANE
IDX_HEADS = 8
IDX_DIM = 64
IDX_PAIRS = IDX_HEADS * IDX_DIM // LANE
TOPK_MAX = 256
DIFF_DIM = HEAD_DIM
ROPE_THETA = 10000.0
LN_EPS = 1e-5
RMS_EPS = 1e-5
ALPHA = (2.0 * DEPTH) ** 0.25
LOGIT_SCALE = HEAD_DIM ** -0.5
NEG_INF = float("-inf")
INT32_MIN = -2 ** 31
BRACKET = 4 << 23
BRACKET_STEPS = 26
LOG2E = 1.4426950408889634
EXP_UNDERFLOW = -104.5

DENOM_ROWS = 16
T_BLOCK = 256
PAIRS_PER_STEP = 4
SCORE_LEAD = 2

Q_BLOCK0, K_BLOCK0, G_BLOCK0, QI_BLOCK0 = 0, N_PAIRS, 2 * N_PAIRS, 3 * N_PAIRS
KI_BLOCK = QI_BLOCK0 + IDX_PAIRS

_NT = (((1,), (1,)), ((), ()))


def _params(n_grid_dims):
    return pltpu.CompilerParams(
        dimension_semantics=("arbitrary",) * n_grid_dims,
        vmem_limit_bytes=V7X_VMEM_LIMIT_BYTES,
    )


def _in_proj_kernel(x_ref, w_ref, cos_ref, sa_ref, sb_ref, main_ref, vt_ref, *f32_ref, ops, chunk, t):
    x = x_ref[...].astype(BF16)
    n_blocks = len(ops)
    tm = x.shape[0]
    for c0 in range(0, n_blocks, chunk):
        nb = min(chunk, n_blocks - c0)
        acc = jnp.dot(x, w_ref[:, c0 * LANE:(c0 + nb) * LANE], preferred_element_type=F32)
        for j in range(nb):
            rope, scale, (dest, slot) = ops[c0 + j]
            y = acc[:, j * LANE:(j + 1) * LANE]
            if rope:
                y = (y * cos_ref[...] + pltpu.roll(y, LANE - 32, 1) * sa_ref[...]
                     + pltpu.roll(y, 32, 1) * sb_ref[...])
            if scale != 1.0:
                y = y * scale
            if dest == "main":
                main_ref[slot] = y.astype(BF16)
            elif dest == "vt":
                y_t = y.T
                for kb in range(tm // t):
                    vt_ref[slot, kb] = y_t[:, kb * t:(kb + 1) * t].astype(BF16)
            else:
                f32_ref[0][slot] = y


def _in_proj(x2d, w, rope_tabs, ops, seq, tm=512, chunk=4):
    m, d = x2d.shape
    t = T_BLOCK
    assert w.shape == (d, len(ops) * LANE) and m % tm == 0 and seq % tm == 0 and tm % t == 0
    n_main = sum(1 for o in ops if o[2][0] == "main")
    n_f32 = sum(1 for o in ops if o[2][0] == "f32")
    s_tiles = seq // tm
    out_shape = [jax.ShapeDtypeStruct((n_main, m, LANE), BF16),
                 jax.ShapeDtypeStruct((N_PAIRS, m // t, LANE, t), BF16)]
    out_specs = [pl.BlockSpec((n_main, tm, LANE), lambda i: (0, i, 0)),
                 pl.BlockSpec((N_PAIRS, tm // t, LANE, t), lambda i: (0, i, 0, 0))]
    if n_f32:
        out_shape.append(jax.ShapeDtypeStruct((n_f32, m, LANE), F32))
        out_specs.append(pl.BlockSpec((n_f32, tm, LANE), lambda i: (0, i, 0)))
    tab_spec = pl.BlockSpec((tm, LANE), lambda i: (i % s_tiles, 0))
    res = pl.pallas_call(
        functools.partial(_in_proj_kernel, ops=ops, chunk=chunk, t=t),
        grid=(m // tm,),
        in_specs=[
            pl.BlockSpec((tm, d), lambda i: (i, 0)),
            pl.BlockSpec((d, len(ops) * LANE), lambda i: (0, 0)),
            tab_spec, tab_spec, tab_spec,
        ],
        out_specs=out_specs,
        out_shape=out_shape,
        compiler_params=_params(1),
        name="in_proj",
    )(x2d, w, *rope_tabs)
    return res if n_f32 else (res[0], res[1], None)


def _out_proj_ln_kernel(o_ref, g_ref, x_ref, w_ref, lng_ref, lnb_ref, out_ref):
    parts = []
    for p in range(N_PAIRS):
        g = g_ref[p].astype(F32)
        gate = g * (1.0 / (1.0 + jnp.exp(-g)))
        parts.append((o_ref[p] * gate).astype(BF16))
    og = jnp.concatenate(parts, axis=1)
    y = jnp.dot(og, w_ref[...], preferred_element_type=F32)
    z = ALPHA * x_ref[...] + y
    mu = jnp.mean(z, axis=-1, keepdims=True)
    zc = z - mu
    var = jnp.mean(zc * zc, axis=-1, keepdims=True)
    out_ref[...] = zc * lax.rsqrt(var + LN_EPS) * lng_ref[...] + lnb_ref[...]


def _out_proj_ln(o, proj, x2d, w_out, ln_g, ln_b, tm=512):
    m, d = x2d.shape
    return pl.pallas_call(
        _out_proj_ln_kernel,
        grid=(m // tm,),
        in_specs=[
            pl.BlockSpec((N_PAIRS, tm, LANE), lambda i: (0, i, 0)),
            pl.BlockSpec((N_PAIRS, tm, LANE), lambda i: (G_BLOCK0 // N_PAIRS, i, 0)),
            pl.BlockSpec((tm, d), lambda i: (i, 0)),
            pl.BlockSpec((INNER, d), lambda i: (0, 0)),
            pl.BlockSpec((1, d), lambda i: (0, 0)),
            pl.BlockSpec((1, d), lambda i: (0, 0)),
        ],
        out_specs=pl.BlockSpec((tm, d), lambda i: (i, 0)),
        out_shape=jax.ShapeDtypeStruct((m, d), F32),
        compiler_params=_params(1),
        name="out_proj_ln",
    )(o, proj, x2d, w_out, ln_g.reshape(1, d), ln_b.reshape(1, d))


def _head_query(q_pair, h):
    is_a = lax.broadcasted_iota(jnp.int32, (1, LANE), 1) < HEAD_DIM
    keep = is_a if h == 0 else jnp.logical_not(is_a)
    return jnp.where(keep, q_pair, jnp.zeros_like(q_pair))


def _key_query_iota(t):
    key = lax.broadcasted_iota(jnp.int32, (t, t), 0)
    qry = lax.broadcasted_iota(jnp.int32, (t, t), 1)
    return key, qry


def _attn_specs(n_q_blocks, seq, t, pg):
    q_spec = pl.BlockSpec((pg, t, LANE), lambda b, h, i: (Q_BLOCK0 // pg + h, b * n_q_blocks + i, 0))
    k_spec = pl.BlockSpec((pg, seq, LANE), lambda b, h, i: (K_BLOCK0 // pg + h, b, 0))
    vt_spec = pl.BlockSpec((pg, seq // t, LANE, t), lambda b, h, i: (h, b, 0, 0))
    o_spec = pl.BlockSpec((pg, t, LANE), lambda b, h, i: (h, b * n_q_blocks + i, 0))
    return q_spec, k_spec, vt_spec, o_spec


def _key_block(k_ref, p, kb, t):
    return k_ref[p, pl.ds(pl.multiple_of(kb * t, t), t), :]


def _split_query_heads(q_ref, qh_ref, n_pairs):
    for p in range(n_pairs):
        for h in range(2):
            qh_ref[2 * p + h] = _head_query(q_ref[p], h)


def _sb_kernel(q_ref, k_ref, vt_ref, o_ref, acc_ref, carry_ref, qh_ref, *, t, pg):
    i = pl.program_id(2)
    _split_query_heads(q_ref, qh_ref, pg)
    key, qry = _key_query_iota(t)
    before = key < qry
    later_keys = jnp.where(qry > key, 1.0, 0.0).astype(BF16)
    acc_ref[...] = jnp.zeros_like(acc_ref)
    carry_ref[...] = jnp.zeros_like(carry_ref)

    def block_step(kb, diagonal):
        zs = [lax.dot_general(_key_block(k_ref, hh // 2, kb, t), qh_ref[hh], _NT, preferred_element_type=F32)
              for hh in range(2 * pg)]
        pre, cum = [], []
        for hh in range(2 * pg):
            z = zs[hh]
            soft = jnp.log(1.0 + jnp.exp2(jnp.abs(z) * -LOG2E))
            log_sig = jnp.minimum(z, 0.0) - soft
            log_om = log_sig - z
            if diagonal:
                log_om = jnp.where(before, log_om, 0.0)
            hi = log_om.astype(BF16)
            lo = (log_om - hi.astype(F32)).astype(BF16)
            c = (jnp.dot(later_keys, hi, preferred_element_type=F32)
                 + jnp.dot(later_keys, lo, preferred_element_type=F32))
            cum.append(c)
            carry = carry_ref[hh:hh + 1, :]
            pre.append(log_sig + carry)
            carry_ref[hh:hh + 1, :] = carry + c[0:1, :] + log_om[0:1, :]
        for hh in range(2 * pg):
            a = jnp.exp(pre[hh] + cum[hh])
            if diagonal:
                a = jnp.where(before, a, 0.0)
            h = hh % 2
            acc_ref[hh] += jnp.dot(vt_ref[hh // 2, kb, h * HEAD_DIM:(h + 1) * HEAD_DIM, :], a.astype(BF16),
                                   preferred_element_type=F32)

    def some_weight_left():
        return (jnp.max(carry_ref[...]) >= EXP_UNDERFLOW).astype(jnp.int32)

    block_step(i, True)

    def kb_cond(state):
        j, go = state
        return jnp.logical_and(j < i, go == 1)

    def kb_body(state):
        j, _ = state
        block_step(i - 1 - j, False)
        return j + 1, some_weight_left()

    lax.while_loop(kb_cond, kb_body, (jnp.int32(0), some_weight_left()))
    for p in range(pg):
        o_ref[p] = jnp.concatenate([acc_ref[2 * p], acc_ref[2 * p + 1]], axis=0).T


def _sb_attention(proj, vt, batch, seq, t=T_BLOCK, pg=PAIRS_PER_STEP):
    nq = seq // t
    q_spec, k_spec, vt_spec, o_spec = _attn_specs(nq, seq, t, pg)
    return pl.pallas_call(
        functools.partial(_sb_kernel, t=t, pg=pg),
        grid=(batch, N_PAIRS // pg, nq),
        in_specs=[q_spec, k_spec, vt_spec],
        out_specs=o_spec,
        out_shape=jax.ShapeDtypeStruct((N_PAIRS, batch * seq, LANE), F32),
        scratch_shapes=[pltpu.VMEM((2 * pg, HEAD_DIM, t), F32), pltpu.VMEM((2 * pg, t), F32),
                        pltpu.VMEM((2 * pg, t, LANE), BF16)],
        compiler_params=_params(3),
        name="stick_breaking_attention",
    )(proj, proj, vt)


def _softmax_block(hh, s, s_max, vt_rows, m_ref, acc_ref):
    m_prev = m_ref[hh]
    m_new = jnp.maximum(m_prev, s_max)
    m_safe = jnp.where(m_new == NEG_INF, 0.0, m_new)
    alpha = jnp.exp2(m_prev - m_safe)
    p = jnp.exp2(s - m_safe).astype(BF16)
    lhs = jnp.concatenate([vt_rows, jnp.ones((DENOM_ROWS, vt_rows.shape[1]), BF16)], axis=0)
    acc_ref[hh] = alpha * acc_ref[hh] + jnp.dot(lhs, p, preferred_element_type=F32)
    m_ref[hh] = m_new


def _reset_softmax_state(m_ref, acc_ref):
    m_ref[...] = jnp.full(m_ref.shape, NEG_INF, F32)
    acc_ref[...] = jnp.zeros_like(acc_ref)


def _normalized(acc_ref, hh, n_rows):
    acc = acc_ref[hh]
    return acc[:n_rows] / acc[n_rows:n_rows + 1]


def _score_tile_into(s_ref, smax_ref, qh_ref, k_ref, t, slot, kb, hh, bias):
    s = lax.dot_general(_key_block(k_ref, hh // 2, kb, t), qh_ref[hh], _NT, preferred_element_type=F32)
    if bias is not None:
        s = s + bias
    s_ref[slot, hh] = s
    smax_ref[slot, hh] = jnp.max(s, axis=0, keepdims=True)


def _pipelined_key_loop(i, n_heads, score_into, consume, bias_of=None):
    def step(next_slot, next_kb, cur_slot, cur_kb, diagonal):
        bias = None if (bias_of is None or next_slot is None) else bias_of(next_kb)
        for h in range(n_heads + SCORE_LEAD):
            if next_slot is not None and h < n_heads:
                score_into(next_slot, next_kb, h, bias)
            if cur_slot is not None and h >= SCORE_LEAD:
                consume(cur_slot, cur_kb, h - SCORE_LEAD, diagonal)

    step(0, 0, None, None, False)

    def pair_body(jj, _):
        kb = 2 * jj
        step(1, kb + 1, 0, kb, False)
        step(0, kb + 2, 1, kb + 1, False)
        return 0

    lax.fori_loop(0, i // 2, pair_body, 0)

    @pl.when(i % 2 == 1)
    def _():
        step(1, i, 0, i - 1, False)

    step(None, None, i % 2, i, True)


def _diff_kernel(lam_ref, subg_ref, q_ref, k_ref, vt_ref, o_ref, m_ref, acc_ref, s_ref, smax_ref, qh_ref, *, t, pg,
                 lambda_init):
    i = pl.program_id(2)
    _split_query_heads(q_ref, qh_ref, pg)
    key, qry = _key_query_iota(t)
    adm = (key // CHUNK) <= (qry // CHUNK)
    _reset_softmax_state(m_ref, acc_ref)

    def consume(slot, kb, hh, diagonal):
        s = s_ref[slot, hh]
        if diagonal:
            s = jnp.where(adm, s, NEG_INF)
            s_max = jnp.max(s, axis=0, keepdims=True)
        else:
            s_max = smax_ref[slot, hh]
        _softmax_block(hh, s, s_max, vt_ref[hh // 2, kb], m_ref, acc_ref)

    _pipelined_key_loop(i, 2 * pg, functools.partial(_score_tile_into, s_ref, smax_ref, qh_ref, k_ref, t), consume)

    lp = lam_ref[...]
    lam = (jnp.exp(jnp.sum(lp[0:1] * lp[1:2], axis=-1, keepdims=True))
           - jnp.exp(jnp.sum(lp[2:3] * lp[3:4], axis=-1, keepdims=True)) + lambda_init)
    for p in range(pg):
        d = _normalized(acc_ref, 2 * p, LANE) - lam * _normalized(acc_ref, 2 * p + 1, LANE)
        ms = jnp.mean(d * d, axis=0, keepdims=True)
        o_ref[p] = (d * lax.rsqrt(ms + RMS_EPS)).T * subg_ref[...] * (1.0 - lambda_init)


def _diff_attention(proj, vt, lam_params, sub_g, lambda_init, batch, seq, t=T_BLOCK, pg=PAIRS_PER_STEP):
    nq = seq // t
    q_spec, k_spec, vt_spec, o_spec = _attn_specs(nq, seq, t, pg)
    return pl.pallas_call(
        functools.partial(_diff_kernel, t=t, pg=pg, lambda_init=lambda_init),
        grid=(batch, N_PAIRS // pg, nq),
        in_specs=[
            pl.BlockSpec((4, LANE), lambda b, h, i: (0, 0)),
            pl.BlockSpec((1, LANE), lambda b, h, i: (0, 0)),
            q_spec, k_spec, vt_spec,
        ],
        out_specs=o_spec,
        out_shape=jax.ShapeDtypeStruct((N_PAIRS, batch * seq, LANE), F32),
        scratch_shapes=[pltpu.VMEM((2 * pg, 1, t), F32), pltpu.VMEM((2 * pg, LANE + DENOM_ROWS, t), F32),
                        pltpu.VMEM((2, 2 * pg, t, t), F32), pltpu.VMEM((2, 2 * pg, 1, t), F32),
                        pltpu.VMEM((2 * pg, t, LANE), BF16)],
        compiler_params=_params(3),
        name="differential_attention",
    )(lam_params, sub_g, proj, proj, vt)


def _dsa_attn_kernel(bias_ref, q_ref, k_ref, vt_ref, o_ref, m_ref, acc_ref, s_ref, smax_ref, qh_ref, *, t, pg):
    i = pl.program_id(2)
    _split_query_heads(q_ref, qh_ref, pg)
    _reset_softmax_state(m_ref, acc_ref)

    def consume(slot, kb, hh, diagonal):
        del diagonal
        h = hh % 2
        _softmax_block(hh, s_ref[slot, hh], smax_ref[slot, hh],
                       vt_ref[hh // 2, kb, h * HEAD_DIM:(h + 1) * HEAD_DIM, :], m_ref, acc_ref)

    def bias_of(kb):
        return bias_ref[0, 0, kb].astype(F32)

    _pipelined_key_loop(i, 2 * pg, functools.partial(_score_tile_into, s_ref, smax_ref, qh_ref, k_ref, t), consume,
                        bias_of)
    for p in range(pg):
        o_ref[p] = jnp.concatenate([_normalized(acc_ref, 2 * p, HEAD_DIM),
                                    _normalized(acc_ref, 2 * p + 1, HEAD_DIM)], axis=0).T


def _dsa_attention(proj, vt, bias, batch, seq, t=T_BLOCK, pg=PAIRS_PER_STEP):
    nq = seq // t
    q_spec, k_spec, vt_spec, o_spec = _attn_specs(nq, seq, t, pg)
    return pl.pallas_call(
        functools.partial(_dsa_attn_kernel, t=t, pg=pg),
        grid=(batch, N_PAIRS // pg, nq),
        in_specs=[
            pl.BlockSpec((1, 1, nq, t, t), lambda b, h, i: (b, i, 0, 0, 0)),
            q_spec, k_spec, vt_spec,
        ],
        out_specs=o_spec,
        out_shape=jax.ShapeDtypeStruct((N_PAIRS, batch * seq, LANE), F32),
        scratch_shapes=[pltpu.VMEM((2 * pg, 1, t), F32), pltpu.VMEM((2 * pg, HEAD_DIM + DENOM_ROWS, t), F32),
                        pltpu.VMEM((2, 2 * pg, t, t), F32), pltpu.VMEM((2, 2 * pg, 1, t), F32),
                        pltpu.VMEM((2 * pg, t, LANE), BF16)],
        compiler_params=_params(3),
        name="dsa_sparse_attention",
    )(bias, proj, proj, vt)


def _dsa_select_kernel(qi_ref, ki_ref, wi_ref, bias_ref, key_ref, thr_ref, hi_ref, nge_ref, cnt_ref, run_ref, qh_ref,
                       wb_ref, lmax_ref, *, t, k_top):
    i = pl.program_id(1)
    nq = bias_ref.shape[2]
    n_groups = t // LANE
    r = lax.broadcasted_iota(jnp.int32, (t, t), 0)
    c = lax.broadcasted_iota(jnp.int32, (t, t), 1)
    adm = (c // CHUNK) <= (r // CHUNK)
    earlier_keys = jnp.where(r < c, 1.0, 0.0).astype(BF16)
    wi = wi_ref[0]
    _split_query_heads(qi_ref, qh_ref, IDX_PAIRS)
    for hh in range(IDX_HEADS):
        wb_ref[hh] = jnp.broadcast_to(wi[:, hh:hh + 1], (t, LANE))

    def head_scores(kb):
        ks = ki_ref[0, pl.ds(pl.multiple_of(kb * t, t), t), :]
        return [lax.dot_general(qh_ref[hh], ks, _NT, preferred_element_type=F32) for hh in range(IDX_HEADS)]

    def keys_from(rels, kb, diagonal):
        sc = None
        for hh in range(IDX_HEADS):
            term = jnp.concatenate([jnp.maximum(rels[hh][:, g * LANE:(g + 1) * LANE], 0.0) * wb_ref[hh]
                                    for g in range(n_groups)], axis=1)
            sc = term if sc is None else sc + term
        if diagonal:
            sc = jnp.where(adm, sc, NEG_INF)
        bits = pltpu.bitcast(sc, jnp.int32)
        srt = bits ^ ((bits >> 31) & 0x7FFFFFFF)
        srt = jnp.where(sc == 0.0, 0, srt)
        key_ref[kb] = srt
        for g in range(n_groups):
            lmax_ref[...] = jnp.maximum(lmax_ref[...], srt[:, g * LANE:(g + 1) * LANE])

    def score_pair_body(jj, _):
        first, second = head_scores(2 * jj), head_scores(2 * jj + 1)
        keys_from(first, 2 * jj, False)
        keys_from(second, 2 * jj + 1, False)
        return 0

    lmax_ref[...] = jnp.full(lmax_ref.shape, INT32_MIN, jnp.int32)
    lax.fori_loop(0, i // 2, score_pair_body, 0)

    @pl.when(i % 2 == 1)
    def _():
        keys_from(head_scores(i - 1), i - 1, False)

    keys_from(head_scores(i), i, True)

    def count_where(pred):
        cnt_ref[...] = jnp.zeros_like(cnt_ref)

        def body(kb, _):
            kk = key_ref[kb]
            part = cnt_ref[...]
            for g in range(n_groups):
                part = part + jnp.where(pred(kk[:, g * LANE:(g + 1) * LANE]), 1.0, 0.0)
            cnt_ref[...] = part
            return 0

        lax.fori_loop(0, i + 1, body, 0)
        return jnp.sum(cnt_ref[...], axis=-1, keepdims=True)

    row_max = jnp.broadcast_to(jnp.max(lmax_ref[...], axis=-1, keepdims=True), thr_ref.shape)
    guess = jnp.where(row_max >= INT32_MIN + BRACKET, row_max - BRACKET, INT32_MIN)
    n_guess = count_where(lambda kk: kk >= guess)
    guess_ok = n_guess >= float(k_top)
    thr_ref[...] = jnp.where(guess_ok, guess, INT32_MIN)
    nge_ref[...] = jnp.where(guess_ok, n_guess, float(t) * (i + 1).astype(F32)) + jnp.zeros(nge_ref.shape, F32)
    hi_ref[...] = row_max + 1

    def bis_body(_, carry):
        lo, hi = thr_ref[...], hi_ref[...]
        mid = lo + lax.shift_right_logical(hi - lo, 1)
        n_ge = count_where(lambda kk: kk >= mid)
        up = n_ge >= float(k_top)
        thr_ref[...] = jnp.where(up, mid, lo)
        hi_ref[...] = jnp.where(up, hi, mid)
        nge_ref[...] = jnp.where(up, n_ge, nge_ref[...])
        return carry

    all_guessed = jnp.min(jnp.where(guess_ok, 1, 0)) == 1
    lax.fori_loop(0, jnp.where(all_guessed, BRACKET_STEPS, 32), bis_body, 0)
    thr = thr_ref[...]

    surplus_ties = jnp.max(nge_ref[...]) > float(k_top)

    def plain_block(kb, diagonal):
        kk = key_ref[kb]
        bias = jnp.concatenate(
            [jnp.where(kk[:, g * LANE:(g + 1) * LANE] >= thr, 0.0, NEG_INF) for g in range(n_groups)], axis=1)
        if diagonal:
            bias = jnp.where(adm, bias, NEG_INF)
        bias_ref[0, 0, kb] = bias.T.astype(BF16)

    @pl.when(jnp.logical_not(surplus_ties))
    def _():
        def plain_body(kb, _):
            plain_block(kb, False)
            return 0

        lax.fori_loop(0, i, plain_body, 0)
        plain_block(i, True)

    def tie_block(kb, diagonal, need):
        kk = key_ref[kb]
        eq = jnp.concatenate(
            [jnp.where(kk[:, g * LANE:(g + 1) * LANE] == thr, 1.0, 0.0) for g in range(n_groups)], axis=1)
        rank = jnp.dot(eq.astype(BF16), earlier_keys, preferred_element_type=F32)
        parts = []
        for g in range(n_groups):
            sl = slice(g * LANE, (g + 1) * LANE)
            tie_ok = (rank[:, sl] + run_ref[...]) < need
            parts.append(jnp.where(kk[:, sl] > thr, 0.0,
                                   jnp.where(kk[:, sl] == thr, jnp.where(tie_ok, 0.0, NEG_INF), NEG_INF)))
        bias = jnp.concatenate(parts, axis=1)
        if diagonal:
            bias = jnp.where(adm, bias, NEG_INF)
        bias_ref[0, 0, kb] = bias.T.astype(BF16)
        run_ref[...] = run_ref[...] + jnp.sum(eq, axis=-1, keepdims=True)

    @pl.when(surplus_ties)
    def _():
        need = float(k_top) - count_where(lambda kk: kk > thr)
        run_ref[...] = jnp.zeros_like(run_ref)

        def tie_body(kb, _):
            tie_block(kb, False, need)
            return 0

        lax.fori_loop(0, i, tie_body, 0)
        tie_block(i, True, need)

    def fill_body(kb, _):
        bias_ref[0, 0, kb] = jnp.full((t, t), NEG_INF, BF16)
        return 0

    lax.fori_loop(i + 1, nq, fill_body, 0)


def _dsa_select(proj, wi, batch, seq, t=T_BLOCK):
    nq = seq // t
    k_top = min(TOPK_MAX, seq // 4)
    return pl.pallas_call(
        functools.partial(_dsa_select_kernel, t=t, k_top=k_top),
        grid=(batch, nq),
        in_specs=[
            pl.BlockSpec((IDX_PAIRS, t, LANE), lambda b, i: (QI_BLOCK0 // IDX_PAIRS, b * nq + i, 0)),
            pl.BlockSpec((1, seq, LANE), lambda b, i: (KI_BLOCK, b, 0)),
            pl.BlockSpec((1, t, LANE), lambda b, i: (0, b * nq + i, 0)),
        ],
        out_specs=pl.BlockSpec((1, 1, nq, t, t), lambda b, i: (b, i, 0, 0, 0)),
        out_shape=jax.ShapeDtypeStruct((batch, nq, nq, t, t), BF16),
        scratch_shapes=[
            pltpu.VMEM((nq, t, t), jnp.int32),
            pltpu.VMEM((t, LANE), jnp.int32),
            pltpu.VMEM((t, LANE), jnp.int32),
            pltpu.VMEM((t, LANE), F32),
            pltpu.VMEM((t, LANE), F32),
            pltpu.VMEM((t, LANE), F32),
            pltpu.VMEM((IDX_HEADS, t, LANE), BF16),
            pltpu.VMEM((IDX_HEADS, t, LANE), F32),
            pltpu.VMEM((t, LANE), jnp.int32),
        ],
        compiler_params=_params(2),
        name="dsa_indexer_select",
    )(proj, proj, wi)


def _rope_tables(seq):
    inv = 1.0 / (ROPE_THETA ** (jnp.arange(0, HEAD_DIM, 2, dtype=F32) / HEAD_DIM))
    ang = jnp.arange(seq, dtype=F32)[:, None] * inv[None, :]
    cos, sin = jnp.cos(ang), jnp.sin(ang)
    zero = jnp.zeros_like(sin)
    n_heads = LANE // HEAD_DIM
    cos_t = jnp.tile(cos, (1, 2 * n_heads))
    sin_first = jnp.tile(jnp.concatenate([-sin, zero], axis=1), (1, n_heads))
    sin_second = jnp.tile(jnp.concatenate([zero, sin], axis=1), (1, n_heads))
    return cos_t, sin_first, sin_second


def _stream_ops(q_rope, k_rope, q_scale):
    q_ops = tuple((q_rope, q_scale, ("main", Q_BLOCK0 + p)) for p in range(N_PAIRS))
    k_ops = tuple((k_rope, 1.0, ("main", K_BLOCK0 + p)) for p in range(N_PAIRS))
    v_ops = tuple((False, 1.0, ("vt", p)) for p in range(N_PAIRS))
    g_ops = tuple((False, 1.0, ("main", G_BLOCK0 + p)) for p in range(N_PAIRS))
    return q_ops + k_ops + v_ops + g_ops


def _one_layer(layer, x2d, params, rope_tabs, batch, seq):
    d = x2d.shape[1]
    mixer, j = layer % N_MIXERS, layer // N_MIXERS
    main = 4 * INNER
    if mixer == 0:
        w = params["w_in_a"][j]
        w_idx_q = w[:, main:main + IDX_HEADS * IDX_DIM]
        w_idx_w = w[:, main + IDX_HEADS * IDX_DIM:main + IDX_HEADS * IDX_DIM + IDX_HEADS]
        w_idx_k = w[:, main + IDX_HEADS * IDX_DIM + IDX_HEADS:]
        w_pad = jnp.concatenate(
            [w[:, :main], w_idx_q, w_idx_k, w_idx_k, w_idx_w, jnp.zeros((d, LANE - IDX_HEADS), w.dtype)],
            axis=1).astype(BF16)
        ops = (_stream_ops(True, True, LOGIT_SCALE * LOG2E)
               + tuple((True, IDX_DIM ** -0.5, ("main", QI_BLOCK0 + p)) for p in range(IDX_PAIRS))
               + ((True, 1.0, ("main", KI_BLOCK)), (False, IDX_HEADS ** -0.5, ("f32", 0))))
        proj, vt, wi = _in_proj(x2d, w_pad, rope_tabs, ops, seq)
        bias = _dsa_select(proj, wi, batch, seq)
        o = _dsa_attention(proj, vt, bias, batch, seq)
        w_out = params["w_out_a"][j]
    elif mixer == 1:
        proj, vt, _ = _in_proj(x2d, params["w_in_b"][j].astype(BF16), rope_tabs, _stream_ops(False, False, LOGIT_SCALE), seq)
        o = _sb_attention(proj, vt, batch, seq)
        w_out = params["w_out_b"][j]
    else:
        lambda_init = 0.8 - 0.6 * math.exp(-0.3 * layer)
        proj, vt, _ = _in_proj(x2d, params["w_in_c"][j].astype(BF16), rope_tabs, _stream_ops(True, True, LOGIT_SCALE * LOG2E),
                               seq)
        lam_rows = jnp.stack([params["lambda_q1"][j], params["lambda_k1"][j],
                              params["lambda_q2"][j], params["lambda_k2"][j]]).astype(F32)
        lam_params = jnp.pad(lam_rows, ((0, 0), (0, LANE - DIFF_DIM)))
        sub_g = params["subln_g"][j].reshape(1, 2 * DIFF_DIM).astype(F32)
        o = _diff_attention(proj, vt, lam_params, sub_g, lambda_init, batch, seq)
        w_out = params["w_out_c"][j]
    return _out_proj_ln(o, proj, x2d, w_out.astype(BF16), params["ln_g"][layer], params["ln_b"][layer])


def kernel(x, w_in_a, w_out_a, w_in_b, w_out_b, w_in_c, w_out_c,
           lambda_q1, lambda_k1, lambda_q2, lambda_k2, subln_g, ln_g, ln_b):
    batch, seq, d = x.shape
    params = dict(w_in_a=w_in_a, w_out_a=w_out_a, w_in_b=w_in_b, w_out_b=w_out_b, w_in_c=w_in_c, w_out_c=w_out_c,
                  lambda_q1=lambda_q1, lambda_k1=lambda_k1, lambda_q2=lambda_q2, lambda_k2=lambda_k2,
                  subln_g=subln_g, ln_g=ln_g, ln_b=ln_b)
    x2d = x.reshape(batch * seq, d)
    rope_tabs = _rope_tables(seq)
    for layer in range(DEPTH):
        x2d = _one_layer(layer, x2d, params, rope_tabs, batch, seq)
    return x2d.reshape(batch, seq, d)
```

```python
import functools
import math

import jax
import jax.numpy as jnp
from jax import lax
from jax.experimental import pallas as pl
from jax.experimental.pallas import tpu as pltpu

F32 = jnp.float32
BF16 = jnp.bfloat16

LANE = 128
V7X_VMEM_LIMIT_BYTES = 56 * 1024 * 1024

D_MODEL = 1024
DEPTH = 4
CHUNK = 64
N_MIXERS = 3
N_HEADS = 16
HEAD_DIM = 64
INNER = N_HEADS * HEAD_DIM
N_PAIRS = INNER // LANE
IDX_HEADS = 8
IDX_DIM = 64
IDX_PAIRS = IDX_HEADS * IDX_DIM // LANE
TOPK_MAX = 256
DIFF_DIM = HEAD_DIM
ROPE_THETA = 10000.0
LN_EPS = 1e-5
RMS_EPS = 1e-5
ALPHA = (2.0 * DEPTH) ** 0.25
LOGIT_SCALE = HEAD_DIM ** -0.5
NEG_INF = float("-inf")
INT32_MIN = -2 ** 31
BRACKET = (1 << 25) - 1
BRACKET_STEPS = 25
LOG2E = 1.4426950408889634
EXP_UNDERFLOW = -104.5

DENOM_ROWS = 16
T_BLOCK = 256
PAIRS_PER_STEP = 4
SCORE_LEAD = 2

Q_BLOCK0, K_BLOCK0, G_BLOCK0, QI_BLOCK0 = 0, N_PAIRS, 2 * N_PAIRS, 3 * N_PAIRS
KI_BLOCK = QI_BLOCK0 + IDX_PAIRS

_NT = (((1,), (1,)), ((), ()))


def _params(n_grid_dims):
    return pltpu.CompilerParams(
        dimension_semantics=("arbitrary",) * n_grid_dims,
        vmem_limit_bytes=V7X_VMEM_LIMIT_BYTES,
    )


def _in_proj_kernel(x_ref, w_ref, cos_ref, sa_ref, sb_ref, main_ref, vt_ref, *f32_ref, ops, chunk, t):
    x = x_ref[...].astype(BF16)
    n_blocks = len(ops)
    tm = x.shape[0]
    for c0 in range(0, n_blocks, chunk):
        nb = min(chunk, n_blocks - c0)
        acc = jnp.dot(x, w_ref[:, c0 * LANE:(c0 + nb) * LANE], preferred_element_type=F32)
        for j in range(nb):
            rope, scale, (dest, slot) = ops[c0 + j]
            y = acc[:, j * LANE:(j + 1) * LANE]
            if rope:
                y = (y * cos_ref[...] + pltpu.roll(y, LANE - 32, 1) * sa_ref[...]
                     + pltpu.roll(y, 32, 1) * sb_ref[...])
            if scale != 1.0:
                y = y * scale
            if dest == "main":
                main_ref[slot] = y.astype(BF16)
            elif dest == "vt":
                y_t = y.T
                for kb in range(tm // t):
                    vt_ref[slot, kb] = y_t[:, kb * t:(kb + 1) * t].astype(BF16)
            else:
                f32_ref[0][slot] = y


def _in_proj(x2d, w, rope_tabs, ops, seq, tm=512, chunk=4):
    m, d = x2d.shape
    t = T_BLOCK
    assert w.shape == (d, len(ops) * LANE) and m % tm == 0 and seq % tm == 0 and tm % t == 0
    n_main = sum(1 for o in ops if o[2][0] == "main")
    n_f32 = sum(1 for o in ops if o[2][0] == "f32")
    s_tiles = seq // tm
    out_shape = [jax.ShapeDtypeStruct((n_main, m, LANE), BF16),
                 jax.ShapeDtypeStruct((N_PAIRS, m // t, LANE, t), BF16)]
    out_specs = [pl.BlockSpec((n_main, tm, LANE), lambda i: (0, i, 0)),
                 pl.BlockSpec((N_PAIRS, tm // t, LANE, t), lambda i: (0, i, 0, 0))]
    if n_f32:
        out_shape.append(jax.ShapeDtypeStruct((n_f32, m, LANE), F32))
        out_specs.append(pl.BlockSpec((n_f32, tm, LANE), lambda i: (0, i, 0)))
    tab_spec = pl.BlockSpec((tm, LANE), lambda i: (i % s_tiles, 0))
    res = pl.pallas_call(
        functools.partial(_in_proj_kernel, ops=ops, chunk=chunk, t=t),
        grid=(m // tm,),
        in_specs=[
            pl.BlockSpec((tm, d), lambda i: (i, 0)),
            pl.BlockSpec((d, len(ops) * LANE), lambda i: (0, 0)),
            tab_spec, tab_spec, tab_spec,
        ],
        out_specs=out_specs,
        out_shape=out_shape,
        compiler_params=_params(1),
        name="in_proj",
    )(x2d, w, *rope_tabs)
    return res if n_f32 else (res[0], res[1], None)


def _out_proj_ln_kernel(o_ref, g_ref, x_ref, w_ref, lng_ref, lnb_ref, out_ref):
    parts = []
    for p in range(N_PAIRS):
        g = g_ref[p].astype(F32)
        gate = g * (1.0 / (1.0 + jnp.exp(-g)))
        parts.append((o_ref[p] * gate).astype(BF16))
    og = jnp.concatenate(parts, axis=1)
    y = jnp.dot(og, w_ref[...], preferred_element_type=F32)
    z = ALPHA * x_ref[...] + y
    mu = jnp.mean(z, axis=-1, keepdims=True)
    zc = z - mu
    var = jnp.mean(zc * zc, axis=-1, keepdims=True)
    out_ref[...] = zc * lax.rsqrt(var + LN_EPS) * lng_ref[...] + lnb_ref[...]


def _out_proj_ln(o, proj, x2d, w_out, ln_g, ln_b, tm=512):
    m, d = x2d.shape
    return pl.pallas_call(
        _out_proj_ln_kernel,
        grid=(m // tm,),
        in_specs=[
            pl.BlockSpec((N_PAIRS, tm, LANE), lambda i: (0, i, 0)),
            pl.BlockSpec((N_PAIRS, tm, LANE), lambda i: (G_BLOCK0 // N_PAIRS, i, 0)),
            pl.BlockSpec((tm, d), lambda i: (i, 0)),
            pl.BlockSpec((INNER, d), lambda i: (0, 0)),
            pl.BlockSpec((1, d), lambda i: (0, 0)),
            pl.BlockSpec((1, d), lambda i: (0, 0)),
        ],
        out_specs=pl.BlockSpec((tm, d), lambda i: (i, 0)),
        out_shape=jax.ShapeDtypeStruct((m, d), F32),
        compiler_params=_params(1),
        name="out_proj_ln",
    )(o, proj, x2d, w_out, ln_g.reshape(1, d), ln_b.reshape(1, d))


def _head_query(q_pair, h):
    is_a = lax.broadcasted_iota(jnp.int32, (1, LANE), 1) < HEAD_DIM
    keep = is_a if h == 0 else jnp.logical_not(is_a)
    return jnp.where(keep, q_pair, jnp.zeros_like(q_pair))


def _key_query_iota(t):
    key = lax.broadcasted_iota(jnp.int32, (t, t), 0)
    qry = lax.broadcasted_iota(jnp.int32, (t, t), 1)
    return key, qry


def _attn_specs(n_q_blocks, seq, t, pg):
    q_spec = pl.BlockSpec((pg, t, LANE), lambda b, h, i: (Q_BLOCK0 // pg + h, b * n_q_blocks + i, 0))
    k_spec = pl.BlockSpec((pg, seq, LANE), lambda b, h, i: (K_BLOCK0 // pg + h, b, 0))
    vt_spec = pl.BlockSpec((pg, seq // t, LANE, t), lambda b, h, i: (h, b, 0, 0))
    o_spec = pl.BlockSpec((pg, t, LANE), lambda b, h, i: (h, b * n_q_blocks + i, 0))
    return q_spec, k_spec, vt_spec, o_spec


def _key_block(k_ref, p, kb, t):
    return k_ref[p, pl.ds(pl.multiple_of(kb * t, t), t), :]


def _split_query_heads(q_ref, qh_ref, n_pairs):
    for p in range(n_pairs):
        for h in range(2):
            qh_ref[2 * p + h] = _head_query(q_ref[p], h)


def _sb_kernel(q_ref, k_ref, vt_ref, o_ref, acc_ref, carry_ref, qh_ref, *, t, pg):
    i = pl.program_id(2)
    _split_query_heads(q_ref, qh_ref, pg)
    key, qry = _key_query_iota(t)
    before = key < qry
    later_keys = jnp.where(qry > key, 1.0, 0.0).astype(BF16)
    acc_ref[...] = jnp.zeros_like(acc_ref)
    carry_ref[...] = jnp.zeros_like(carry_ref)

    def block_step(kb, diagonal):
        zs = [lax.dot_general(_key_block(k_ref, hh // 2, kb, t), qh_ref[hh], _NT, preferred_element_type=F32)
              for hh in range(2 * pg)]
        pre, cum = [], []
        for hh in range(2 * pg):
            z = zs[hh]
            soft = jnp.log(1.0 + jnp.exp2(jnp.abs(z) * -LOG2E))
            log_sig = jnp.minimum(z, 0.0) - soft
            log_om = log_sig - z
            if diagonal:
                log_om = jnp.where(before, log_om, 0.0)
            hi = log_om.astype(BF16)
            lo = (log_om - hi.astype(F32)).astype(BF16)
            c = (jnp.dot(later_keys, hi, preferred_element_type=F32)
                 + jnp.dot(later_keys, lo, preferred_element_type=F32))
            cum.append(c)
            carry = carry_ref[hh:hh + 1, :]
            pre.append(log_sig + carry)
            carry_ref[hh:hh + 1, :] = carry + c[0:1, :] + log_om[0:1, :]
        for hh in range(2 * pg):
            a = jnp.exp(pre[hh] + cum[hh])
            if diagonal:
                a = jnp.where(before, a, 0.0)
            h = hh % 2
            acc_ref[hh] += jnp.dot(vt_ref[hh // 2, kb, h * HEAD_DIM:(h + 1) * HEAD_DIM, :], a.astype(BF16),
                                   preferred_element_type=F32)

    def some_weight_left():
        return (jnp.max(carry_ref[...]) >= EXP_UNDERFLOW).astype(jnp.int32)

    block_step(i, True)

    def kb_cond(state):
        j, go = state
        return jnp.logical_and(j < i, go == 1)

    def kb_body(state):
        j, _ = state
        block_step(i - 1 - j, False)
        return j + 1, some_weight_left()

    lax.while_loop(kb_cond, kb_body, (jnp.int32(0), some_weight_left()))
    for p in range(pg):
        o_ref[p] = jnp.concatenate([acc_ref[2 * p], acc_ref[2 * p + 1]], axis=0).T


def _sb_attention(proj, vt, batch, seq, t=T_BLOCK, pg=PAIRS_PER_STEP):
    nq = seq // t
    q_spec, k_spec, vt_spec, o_spec = _attn_specs(nq, seq, t, pg)
    return pl.pallas_call(
        functools.partial(_sb_kernel, t=t, pg=pg),
        grid=(batch, N_PAIRS // pg, nq),
        in_specs=[q_spec, k_spec, vt_spec],
        out_specs=o_spec,
        out_shape=jax.ShapeDtypeStruct((N_PAIRS, batch * seq, LANE), F32),
        scratch_shapes=[pltpu.VMEM((2 * pg, HEAD_DIM, t), F32), pltpu.VMEM((2 * pg, t), F32),
                        pltpu.VMEM((2 * pg, t, LANE), BF16)],
        compiler_params=_params(3),
        name="stick_breaking_attention",
    )(proj, proj, vt)


def _softmax_block(hh, s, s_max, vt_rows, m_ref, acc_ref):
    m_prev = m_ref[hh]
    m_new = jnp.maximum(m_prev, s_max)
    m_safe = jnp.where(m_new == NEG_INF, 0.0, m_new)
    alpha = jnp.exp2(m_prev - m_safe)
    p = jnp.exp2(s - m_safe).astype(BF16)
    lhs = jnp.concatenate([vt_rows, jnp.ones((DENOM_ROWS, vt_rows.shape[1]), BF16)], axis=0)
    acc_ref[hh] = alpha * acc_ref[hh] + jnp.dot(lhs, p, preferred_element_type=F32)
    m_ref[hh] = m_new


def _reset_softmax_state(m_ref, acc_ref):
    m_ref[...] = jnp.full(m_ref.shape, NEG_INF, F32)
    acc_ref[...] = jnp.zeros_like(acc_ref)


def _normalized(acc_ref, hh, n_rows):
    acc = acc_ref[hh]
    return acc[:n_rows] / acc[n_rows:n_rows + 1]


def _score_tile_into(s_ref, smax_ref, qh_ref, k_ref, t, slot, kb, hh, bias):
    s = lax.dot_general(_key_block(k_ref, hh // 2, kb, t), qh_ref[hh], _NT, preferred_element_type=F32)
    if bias is not None:
        s = s + bias
    s_ref[slot, hh] = s
    smax_ref[slot, hh] = jnp.max(s, axis=0, keepdims=True)


def _pipelined_key_loop(i, n_heads, score_into, consume, bias_of=None):
    def step(next_slot, next_kb, cur_slot, cur_kb, diagonal):
        bias = None if (bias_of is None or next_slot is None) else bias_of(next_kb)
        for h in range(n_heads + SCORE_LEAD):
            if next_slot is not None and h < n_heads:
                score_into(next_slot, next_kb, h, bias)
            if cur_slot is not None and h >= SCORE_LEAD:
                consume(cur_slot, cur_kb, h - SCORE_LEAD, diagonal)

    step(0, 0, None, None, False)

    def pair_body(jj, _):
        kb = 2 * jj
        step(1, kb + 1, 0, kb, False)
        step(0, kb + 2, 1, kb + 1, False)
        return 0

    lax.fori_loop(0, i // 2, pair_body, 0)

    @pl.when(i % 2 == 1)
    def _():
        step(1, i, 0, i - 1, False)

    step(None, None, i % 2, i, True)


def _diff_kernel(lam_ref, subg_ref, q_ref, k_ref, vt_ref, o_ref, m_ref, acc_ref, s_ref, smax_ref, qh_ref, *, t, pg,
                 lambda_init):
    i = pl.program_id(2)
    _split_query_heads(q_ref, qh_ref, pg)
    key, qry = _key_query_iota(t)
    adm = (key // CHUNK) <= (qry // CHUNK)
    _reset_softmax_state(m_ref, acc_ref)

    def consume(slot, kb, hh, diagonal):
        s = s_ref[slot, hh]
        if diagonal:
            s = jnp.where(adm, s, NEG_INF)
            s_max = jnp.max(s, axis=0, keepdims=True)
        else:
            s_max = smax_ref[slot, hh]
        _softmax_block(hh, s, s_max, vt_ref[hh // 2, kb], m_ref, acc_ref)

    _pipelined_key_loop(i, 2 * pg, functools.partial(_score_tile_into, s_ref, smax_ref, qh_ref, k_ref, t), consume)

    lp = lam_ref[...]
    lam = (jnp.exp(jnp.sum(lp[0:1] * lp[1:2], axis=-1, keepdims=True))
           - jnp.exp(jnp.sum(lp[2:3] * lp[3:4], axis=-1, keepdims=True)) + lambda_init)
    for p in range(pg):
        d = _normalized(acc_ref, 2 * p, LANE) - lam * _normalized(acc_ref, 2 * p + 1, LANE)
        ms = jnp.mean(d * d, axis=0, keepdims=True)
        o_ref[p] = (d * lax.rsqrt(ms + RMS_EPS)).T * subg_ref[...] * (1.0 - lambda_init)


def _diff_attention(proj, vt, lam_params, sub_g, lambda_init, batch, seq, t=T_BLOCK, pg=PAIRS_PER_STEP):
    nq = seq // t
    q_spec, k_spec, vt_spec, o_spec = _attn_specs(nq, seq, t, pg)
    return pl.pallas_call(
        functools.partial(_diff_kernel, t=t, pg=pg, lambda_init=lambda_init),
        grid=(batch, N_PAIRS // pg, nq),
        in_specs=[
            pl.BlockSpec((4, LANE), lambda b, h, i: (0, 0)),
            pl.BlockSpec((1, LANE), lambda b, h, i: (0, 0)),
            q_spec, k_spec, vt_spec,
        ],
        out_specs=o_spec,
        out_shape=jax.ShapeDtypeStruct((N_PAIRS, batch * seq, LANE), F32),
        scratch_shapes=[pltpu.VMEM((2 * pg, 1, t), F32), pltpu.VMEM((2 * pg, LANE + DENOM_ROWS, t), F32),
                        pltpu.VMEM((2, 2 * pg, t, t), F32), pltpu.VMEM((2, 2 * pg, 1, t), F32),
                        pltpu.VMEM((2 * pg, t, LANE), BF16)],
        compiler_params=_params(3),
        name="differential_attention",
    )(lam_params, sub_g, proj, proj, vt)


def _dsa_attn_kernel(bias_ref, q_ref, k_ref, vt_ref, o_ref, m_ref, acc_ref, s_ref, smax_ref, qh_ref, *, t, pg):
    i = pl.program_id(2)
    _split_query_heads(q_ref, qh_ref, pg)
    _reset_softmax_state(m_ref, acc_ref)

    def consume(slot, kb, hh, diagonal):
        del diagonal
        h = hh % 2
        _softmax_block(hh, s_ref[slot, hh], smax_ref[slot, hh],
                       vt_ref[hh // 2, kb, h * HEAD_DIM:(h + 1) * HEAD_DIM, :], m_ref, acc_ref)

    def bias_of(kb):
        return bias_ref[0, 0, kb].astype(F32)

    _pipelined_key_loop(i, 2 * pg, functools.partial(_score_tile_into, s_ref, smax_ref, qh_ref, k_ref, t), consume,
                        bias_of)
    for p in range(pg):
        o_ref[p] = jnp.concatenate([_normalized(acc_ref, 2 * p, HEAD_DIM),
                                    _normalized(acc_ref, 2 * p + 1, HEAD_DIM)], axis=0).T


def _dsa_attention(proj, vt, bias, batch, seq, t=T_BLOCK, pg=PAIRS_PER_STEP):
    nq = seq // t
    q_spec, k_spec, vt_spec, o_spec = _attn_specs(nq, seq, t, pg)
    return pl.pallas_call(
        functools.partial(_dsa_attn_kernel, t=t, pg=pg),
        grid=(batch, N_PAIRS // pg, nq),
        in_specs=[
            pl.BlockSpec((1, 1, nq, t, t), lambda b, h, i: (b, i, 0, 0, 0)),
            q_spec, k_spec, vt_spec,
        ],
        out_specs=o_spec,
        out_shape=jax.ShapeDtypeStruct((N_PAIRS, batch * seq, LANE), F32),
        scratch_shapes=[pltpu.VMEM((2 * pg, 1, t), F32), pltpu.VMEM((2 * pg, HEAD_DIM + DENOM_ROWS, t), F32),
                        pltpu.VMEM((2, 2 * pg, t, t), F32), pltpu.VMEM((2, 2 * pg, 1, t), F32),
                        pltpu.VMEM((2 * pg, t, LANE), BF16)],
        compiler_params=_params(3),
        name="dsa_sparse_attention",
    )(bias, proj, proj, vt)


def _dsa_select_kernel(qi_ref, ki_ref, wi_ref, bias_ref, key_ref, thr_ref, hi_ref, nge_ref, cnt_ref, run_ref, qh_ref,
                       wb_ref, lmax_ref, *, t, k_top):
    i = pl.program_id(1)
    nq = bias_ref.shape[2]
    n_groups = t // LANE
    r = lax.broadcasted_iota(jnp.int32, (t, t), 0)
    c = lax.broadcasted_iota(jnp.int32, (t, t), 1)
    adm = (c // CHUNK) <= (r // CHUNK)
    earlier_keys = jnp.where(r < c, 1.0, 0.0).astype(BF16)
    wi = wi_ref[0]
    _split_query_heads(qi_ref, qh_ref, IDX_PAIRS)
    for hh in range(IDX_HEADS):
        wb_ref[hh] = jnp.broadcast_to(wi[:, hh:hh + 1], (t, LANE))

    def head_scores(kb):
        ks = ki_ref[0, pl.ds(pl.multiple_of(kb * t, t), t), :]
        return [lax.dot_general(qh_ref[hh], ks, _NT, preferred_element_type=F32) for hh in range(IDX_HEADS)]

    def keys_from(rels, kb, diagonal):
        sc = None
        for hh in range(IDX_HEADS):
            term = jnp.concatenate([jnp.maximum(rels[hh][:, g * LANE:(g + 1) * LANE], 0.0) * wb_ref[hh]
                                    for g in range(n_groups)], axis=1)
            sc = term if sc is None else sc + term
        if diagonal:
            sc = jnp.where(adm, sc, NEG_INF)
        bits = pltpu.bitcast(sc, jnp.int32)
        srt = bits ^ ((bits >> 31) & 0x7FFFFFFF)
        srt = jnp.where(sc == 0.0, 0, srt)
        key_ref[kb] = srt
        for g in range(n_groups):
            lmax_ref[...] = jnp.maximum(lmax_ref[...], srt[:, g * LANE:(g + 1) * LANE])

    def score_pair_body(jj, _):
        first, second = head_scores(2 * jj), head_scores(2 * jj + 1)
        keys_from(first, 2 * jj, False)
        keys_from(second, 2 * jj + 1, False)
        return 0

    lmax_ref[...] = jnp.full(lmax_ref.shape, INT32_MIN, jnp.int32)
    lax.fori_loop(0, i // 2, score_pair_body, 0)

    @pl.when(i % 2 == 1)
    def _():
        keys_from(head_scores(i - 1), i - 1, False)

    keys_from(head_scores(i), i, True)

    def count_where(pred):
        cnt_ref[...] = jnp.zeros_like(cnt_ref)

        def body(kb, _):
            kk = key_ref[kb]
            part = cnt_ref[...]
            for g in range(n_groups):
                part = part + jnp.where(pred(kk[:, g * LANE:(g + 1) * LANE]), 1.0, 0.0)
            cnt_ref[...] = part
            return 0

        lax.fori_loop(0, i + 1, body, 0)
        return jnp.sum(cnt_ref[...], axis=-1, keepdims=True)

    row_max = jnp.broadcast_to(jnp.max(lmax_ref[...], axis=-1, keepdims=True), thr_ref.shape)
    guess = jnp.where(row_max >= INT32_MIN + BRACKET, row_max - BRACKET, INT32_MIN)
    n_guess = count_where(lambda kk: kk >= guess)
    guess_ok = n_guess >= float(k_top)
    thr_ref[...] = jnp.where(guess_ok, guess, INT32_MIN)
    nge_ref[...] = jnp.where(guess_ok, n_guess, float(t) * (i + 1).astype(F32)) + jnp.zeros(nge_ref.shape, F32)
    hi_ref[...] = row_max + 1

    def bis_body(_, carry):
        lo, hi = thr_ref[...], hi_ref[...]
        mid = lo + lax.shift_right_logical(hi - lo, 1)
        n_ge = count_where(lambda kk: kk >= mid)
        up = n_ge >= float(k_top)
        thr_ref[...] = jnp.where(up, mid, lo)
        hi_ref[...] = jnp.where(up, hi, mid)
        nge_ref[...] = jnp.where(up, n_ge, nge_ref[...])
        return carry

    all_guessed = jnp.min(jnp.where(guess_ok, 1, 0)) == 1
    lax.fori_loop(0, jnp.where(all_guessed, BRACKET_STEPS, 32), bis_body, 0)
    thr = thr_ref[...]

    surplus_ties = jnp.max(nge_ref[...]) > float(k_top)

    def plain_block(kb, diagonal):
        kk = key_ref[kb]
        bias = jnp.concatenate(
            [jnp.where(kk[:, g * LANE:(g + 1) * LANE] >= thr, 0.0, NEG_INF) for g in range(n_groups)], axis=1)
        if diagonal:
            bias = jnp.where(adm, bias, NEG_INF)
        bias_ref[0, 0, kb] = bias.T.astype(BF16)

    @pl.when(jnp.logical_not(surplus_ties))
    def _():
        def plain_body(kb, _):
            plain_block(kb, False)
            return 0

        lax.fori_loop(0, i, plain_body, 0)
        plain_block(i, True)

    def tie_block(kb, diagonal, need):
        kk = key_ref[kb]
        eq = jnp.concatenate(
            [jnp.where(kk[:, g * LANE:(g + 1) * LANE] == thr, 1.0, 0.0) for g in range(n_groups)], axis=1)
        rank = jnp.dot(eq.astype(BF16), earlier_keys, preferred_element_type=F32)
        parts = []
        for g in range(n_groups):
            sl = slice(g * LANE, (g + 1) * LANE)
            tie_ok = (rank[:, sl] + run_ref[...]) < need
            parts.append(jnp.where(kk[:, sl] > thr, 0.0,
                                   jnp.where(kk[:, sl] == thr, jnp.where(tie_ok, 0.0, NEG_INF), NEG_INF)))
        bias = jnp.concatenate(parts, axis=1)
        if diagonal:
            bias = jnp.where(adm, bias, NEG_INF)
        bias_ref[0, 0, kb] = bias.T.astype(BF16)
        run_ref[...] = run_ref[...] + jnp.sum(eq, axis=-1, keepdims=True)

    @pl.when(surplus_ties)
    def _():
        need = float(k_top) - count_where(lambda kk: kk > thr)
        run_ref[...] = jnp.zeros_like(run_ref)

        def tie_body(kb, _):
            tie_block(kb, False, need)
            return 0

        lax.fori_loop(0, i, tie_body, 0)
        tie_block(i, True, need)

    def fill_body(kb, _):
        bias_ref[0, 0, kb] = jnp.full((t, t), NEG_INF, BF16)
        return 0

    lax.fori_loop(i + 1, nq, fill_body, 0)


def _dsa_select(proj, wi, batch, seq, t=T_BLOCK):
    nq = seq // t
    k_top = min(TOPK_MAX, seq // 4)
    return pl.pallas_call(
        functools.partial(_dsa_select_kernel, t=t, k_top=k_top),
        grid=(batch, nq),
        in_specs=[
            pl.BlockSpec((IDX_PAIRS, t, LANE), lambda b, i: (QI_BLOCK0 // IDX_PAIRS, b * nq + i, 0)),
            pl.BlockSpec((1, seq, LANE), lambda b, i: (KI_BLOCK, b, 0)),
            pl.BlockSpec((1, t, LANE), lambda b, i: (0, b * nq + i, 0)),
        ],
        out_specs=pl.BlockSpec((1, 1, nq, t, t), lambda b, i: (b, i, 0, 0, 0)),
        out_shape=jax.ShapeDtypeStruct((batch, nq, nq, t, t), BF16),
        scratch_shapes=[
            pltpu.VMEM((nq, t, t), jnp.int32),
            pltpu.VMEM((t, LANE), jnp.int32),
            pltpu.VMEM((t, LANE), jnp.int32),
            pltpu.VMEM((t, LANE), F32),
            pltpu.VMEM((t, LANE), F32),
            pltpu.VMEM((t, LANE), F32),
            pltpu.VMEM((IDX_HEADS, t, LANE), BF16),
            pltpu.VMEM((IDX_HEADS, t, LANE), F32),
            pltpu.VMEM((t, LANE), jnp.int32),
        ],
        compiler_params=_params(2),
        name="dsa_indexer_select",
    )(proj, proj, wi)


def _rope_tables(seq):
    inv = 1.0 / (ROPE_THETA ** (jnp.arange(0, HEAD_DIM, 2, dtype=F32) / HEAD_DIM))
    ang = jnp.arange(seq, dtype=F32)[:, None] * inv[None, :]
    cos, sin = jnp.cos(ang), jnp.sin(ang)
    zero = jnp.zeros_like(sin)
    n_heads = LANE // HEAD_DIM
    cos_t = jnp.tile(cos, (1, 2 * n_heads))
    sin_first = jnp.tile(jnp.concatenate([-sin, zero], axis=1), (1, n_heads))
    sin_second = jnp.tile(jnp.concatenate([zero, sin], axis=1), (1, n_heads))
    return cos_t, sin_first, sin_second


def _stream_ops(q_rope, k_rope, q_scale):
    q_ops = tuple((q_rope, q_scale, ("main", Q_BLOCK0 + p)) for p in range(N_PAIRS))
    k_ops = tuple((k_rope, 1.0, ("main", K_BLOCK0 + p)) for p in range(N_PAIRS))
    v_ops = tuple((False, 1.0, ("vt", p)) for p in range(N_PAIRS))
    g_ops = tuple((False, 1.0, ("main", G_BLOCK0 + p)) for p in range(N_PAIRS))
    return q_ops + k_ops + v_ops + g_ops


def _one_layer(layer, x2d, params, rope_tabs, batch, seq):
    d = x2d.shape[1]
    mixer, j = layer % N_MIXERS, layer // N_MIXERS
    main = 4 * INNER
    if mixer == 0:
        w = params["w_in_a"][j]
        w_idx_q = w[:, main:main + IDX_HEADS * IDX_DIM]
        w_idx_w = w[:, main + IDX_HEADS * IDX_DIM:main + IDX_HEADS * IDX_DIM + IDX_HEADS]
        w_idx_k = w[:, main + IDX_HEADS * IDX_DIM + IDX_HEADS:]
        w_pad = jnp.concatenate(
            [w[:, :main], w_idx_q, w_idx_k, w_idx_k, w_idx_w, jnp.zeros((d, LANE - IDX_HEADS), w.dtype)],
            axis=1).astype(BF16)
        ops = (_stream_ops(True, True, LOGIT_SCALE * LOG2E)
               + tuple((True, IDX_DIM ** -0.5, ("main", QI_BLOCK0 + p)) for p in range(IDX_PAIRS))
               + ((True, 1.0, ("main", KI_BLOCK)), (False, IDX_HEADS ** -0.5, ("f32", 0))))
        proj, vt, wi = _in_proj(x2d, w_pad, rope_tabs, ops, seq)
        bias = _dsa_select(proj, wi, batch, seq)
        o = _dsa_attention(proj, vt, bias, batch, seq)
        w_out = params["w_out_a"][j]
    elif mixer == 1:
        proj, vt, _ = _in_proj(x2d, params["w_in_b"][j].astype(BF16), rope_tabs, _stream_ops(False, False, LOGIT_SCALE), seq)
        o = _sb_attention(proj, vt, batch, seq)
        w_out = params["w_out_b"][j]
    else:
        lambda_init = 0.8 - 0.6 * math.exp(-0.3 * layer)
        proj, vt, _ = _in_proj(x2d, params["w_in_c"][j].astype(BF16), rope_tabs, _stream_ops(True, True, LOGIT_SCALE * LOG2E),
                               seq)
        lam_rows = jnp.stack([params["lambda_q1"][j], params["lambda_k1"][j],
                              params["lambda_q2"][j], params["lambda_k2"][j]]).astype(F32)
        lam_params = jnp.pad(lam_rows, ((0, 0), (0, LANE - DIFF_DIM)))
        sub_g = params["subln_g"][j].reshape(1, 2 * DIFF_DIM).astype(F32)
        o = _diff_attention(proj, vt, lam_params, sub_g, lambda_init, batch, seq)
        w_out = params["w_out_c"][j]
    return _out_proj_ln(o, proj, x2d, w_out.astype(BF16), params["ln_g"][layer], params["ln_b"][layer])


def kernel(x, w_in_a, w_out_a, w_in_b, w_out_b, w_in_c, w_out_c,
           lambda_q1, lambda_k1, lambda_q2, lambda_k2, subln_g, ln_g, ln_b):
    batch, seq, d = x.shape
    params = dict(w_in_a=w_in_a, w_out_a=w_out_a, w_in_b=w_in_b, w_out_b=w_out_b, w_in_c=w_in_c, w_out_c=w_out_c,
                  lambda_q1=lambda_q1, lambda_k1=lambda_k1, lambda_q2=lambda_q2, lambda_k2=lambda_k2,
                  subln_g=subln_g, ln_g=ln_g, ln_b=ln_b)
    x2d = x.reshape(batch * seq, d)
    rope_tabs = _rope_tables(seq)
    for layer in range(DEPTH):
        x2d = _one_layer(layer, x2d, params, rope_tabs, batch, seq)
    return x2d.reshape(batch, seq, d)
```

```python
import functools
import math

import jax
import jax.numpy as jnp
from jax import lax
from jax.experimental import pallas as pl
from jax.experimental.pallas import tpu as pltpu

F32 = jnp.float32
BF16 = jnp.bfloat16

LANE = 128
V7X_VMEM_LIMIT_BYTES = 56 * 1024 * 1024

D_MODEL = 1024
DEPTH = 4
CHUNK = 64
N_MIXERS = 3
N_HEADS = 16
HEAD_DIM = 64
INNER = N_HEADS * HEAD_DIM
N_PAIRS = INNER // LANE
IDX_HEADS = 8
IDX_DIM = 64
IDX_PAIRS = IDX_HEADS * IDX_DIM // LANE
TOPK_MAX = 256
DIFF_DIM = HEAD_DIM
ROPE_THETA = 10000.0
LN_EPS = 1e-5
RMS_EPS = 1e-5
ALPHA = (2.0 * DEPTH) ** 0.25
LOGIT_SCALE = HEAD_DIM ** -0.5
NEG_INF = float("-inf")
INT32_MIN = -2 ** 31
BRACKET = (1 << 25) - 1
BRACKET_STEPS = 25
LOG2E = 1.4426950408889634
EXP_UNDERFLOW = -104.5

DENOM_ROWS = 16
T_BLOCK = 256
PAIRS_PER_STEP = 4
SCORE_LEAD = 2

Q_BLOCK0, K_BLOCK0, G_BLOCK0, QI_BLOCK0 = 0, N_PAIRS, 2 * N_PAIRS, 3 * N_PAIRS
KI_BLOCK = QI_BLOCK0 + IDX_PAIRS

_NT = (((1,), (1,)), ((), ()))


def _params(n_grid_dims):
    return pltpu.CompilerParams(
        dimension_semantics=("arbitrary",) * n_grid_dims,
        vmem_limit_bytes=V7X_VMEM_LIMIT_BYTES,
    )


def _in_proj_kernel(x_ref, w_ref, cos_ref, sa_ref, sb_ref, main_ref, vt_ref, *f32_ref, ops, chunk, t):
    x = x_ref[...].astype(BF16)
    n_blocks = len(ops)
    tm = x.shape[0]
    for c0 in range(0, n_blocks, chunk):
        nb = min(chunk, n_blocks - c0)
        acc = jnp.dot(x, w_ref[:, c0 * LANE:(c0 + nb) * LANE], preferred_element_type=F32)
        for j in range(nb):
            rope, scale, (dest, slot) = ops[c0 + j]
            y = acc[:, j * LANE:(j + 1) * LANE]
            if rope:
                y = (y * cos_ref[...] + pltpu.roll(y, LANE - 32, 1) * sa_ref[...]
                     + pltpu.roll(y, 32, 1) * sb_ref[...])
            if scale != 1.0:
                y = y * scale
            if dest == "main":
                main_ref[slot] = y.astype(BF16)
            elif dest == "vt":
                y_t = y.T
                for kb in range(tm // t):
                    vt_ref[slot, kb] = y_t[:, kb * t:(kb + 1) * t].astype(BF16)
            else:
                f32_ref[0][slot] = y


def _in_proj(x2d, w, rope_tabs, ops, seq, tm=512, chunk=4):
    m, d = x2d.shape
    t = T_BLOCK
    assert w.shape == (d, len(ops) * LANE) and m % tm == 0 and seq % tm == 0 and tm % t == 0
    n_main = sum(1 for o in ops if o[2][0] == "main")
    n_f32 = sum(1 for o in ops if o[2][0] == "f32")
    s_tiles = seq // tm
    out_shape = [jax.ShapeDtypeStruct((n_main, m, LANE), BF16),
                 jax.ShapeDtypeStruct((N_PAIRS, m // t, LANE, t), BF16)]
    out_specs = [pl.BlockSpec((n_main, tm, LANE), lambda i: (0, i, 0)),
                 pl.BlockSpec((N_PAIRS, tm // t, LANE, t), lambda i: (0, i, 0, 0))]
    if n_f32:
        out_shape.append(jax.ShapeDtypeStruct((n_f32, m, LANE), F32))
        out_specs.append(pl.BlockSpec((n_f32, tm, LANE), lambda i: (0, i, 0)))
    tab_spec = pl.BlockSpec((tm, LANE), lambda i: (i % s_tiles, 0))
    res = pl.pallas_call(
        functools.partial(_in_proj_kernel, ops=ops, chunk=chunk, t=t),
        grid=(m // tm,),
        in_specs=[
            pl.BlockSpec((tm, d), lambda i: (i, 0)),
            pl.BlockSpec((d, len(ops) * LANE), lambda i: (0, 0)),
            tab_spec, tab_spec, tab_spec,
        ],
        out_specs=out_specs,
        out_shape=out_shape,
        compiler_params=_params(1),
        name="in_proj",
    )(x2d, w, *rope_tabs)
    return res if n_f32 else (res[0], res[1], None)


def _out_proj_ln_kernel(o_ref, g_ref, x_ref, w_ref, lng_ref, lnb_ref, out_ref):
    parts = []
    for p in range(N_PAIRS):
        g = g_ref[p].astype(F32)
        gate = g * (1.0 / (1.0 + jnp.exp(-g)))
        parts.append((o_ref[p].astype(F32) * gate).astype(BF16))
    og = jnp.concatenate(parts, axis=1)
    y = jnp.dot(og, w_ref[...], preferred_element_type=F32)
    z = ALPHA * x_ref[...] + y
    mu = jnp.mean(z, axis=-1, keepdims=True)
    zc = z - mu
    var = jnp.mean(zc * zc, axis=-1, keepdims=True)
    out_ref[...] = zc * lax.rsqrt(var + LN_EPS) * lng_ref[...] + lnb_ref[...]


def _out_proj_ln(o, proj, x2d, w_out, ln_g, ln_b, tm=512):
    m, d = x2d.shape
    return pl.pallas_call(
        _out_proj_ln_kernel,
        grid=(m // tm,),
        in_specs=[
            pl.BlockSpec((N_PAIRS, tm, LANE), lambda i: (0, i, 0)),
            pl.BlockSpec((N_PAIRS, tm, LANE), lambda i: (G_BLOCK0 // N_PAIRS, i, 0)),
            pl.BlockSpec((tm, d), lambda i: (i, 0)),
            pl.BlockSpec((INNER, d), lambda i: (0, 0)),
            pl.BlockSpec((1, d), lambda i: (0, 0)),
            pl.BlockSpec((1, d), lambda i: (0, 0)),
        ],
        out_specs=pl.BlockSpec((tm, d), lambda i: (i, 0)),
        out_shape=jax.ShapeDtypeStruct((m, d), F32),
        compiler_params=_params(1),
        name="out_proj_ln",
    )(o, proj, x2d, w_out, ln_g.reshape(1, d), ln_b.reshape(1, d))


def _head_query(q_pair, h):
    is_a = lax.broadcasted_iota(jnp.int32, (1, LANE), 1) < HEAD_DIM
    keep = is_a if h == 0 else jnp.logical_not(is_a)
    return jnp.where(keep, q_pair, jnp.zeros_like(q_pair))


def _key_query_iota(t):
    key = lax.broadcasted_iota(jnp.int32, (t, t), 0)
    qry = lax.broadcasted_iota(jnp.int32, (t, t), 1)
    return key, qry


def _attn_specs(n_q_blocks, seq, t, pg):
    q_spec = pl.BlockSpec((pg, t, LANE), lambda b, h, i: (Q_BLOCK0 // pg + h, b * n_q_blocks + i, 0))
    k_spec = pl.BlockSpec((pg, seq, LANE), lambda b, h, i: (K_BLOCK0 // pg + h, b, 0))
    vt_spec = pl.BlockSpec((pg, seq // t, LANE, t), lambda b, h, i: (h, b, 0, 0))
    o_spec = pl.BlockSpec((pg, t, LANE), lambda b, h, i: (h, b * n_q_blocks + i, 0))
    return q_spec, k_spec, vt_spec, o_spec


def _key_block(k_ref, p, kb, t):
    return k_ref[p, pl.ds(pl.multiple_of(kb * t, t), t), :]


def _split_query_heads(q_ref, qh_ref, n_pairs):
    for p in range(n_pairs):
        for h in range(2):
            qh_ref[2 * p + h] = _head_query(q_ref[p], h)


def _sb_kernel(q_ref, k_ref, vt_ref, o_ref, acc_ref, carry_ref, qh_ref, *, t, pg):
    i = pl.program_id(2)
    _split_query_heads(q_ref, qh_ref, pg)
    key, qry = _key_query_iota(t)
    before = key < qry
    later_keys = jnp.where(qry > key, 1.0, 0.0).astype(BF16)
    acc_ref[...] = jnp.zeros_like(acc_ref)
    carry_ref[...] = jnp.zeros_like(carry_ref)

    def block_step(kb, diagonal):
        zs = [lax.dot_general(_key_block(k_ref, hh // 2, kb, t), qh_ref[hh], _NT, preferred_element_type=F32)
              for hh in range(2 * pg)]
        pre, cum = [], []
        for hh in range(2 * pg):
            z = zs[hh]
            soft = jnp.log(1.0 + jnp.exp2(jnp.abs(z) * -LOG2E))
            log_sig = jnp.minimum(z, 0.0) - soft
            log_om = log_sig - z
            if diagonal:
                log_om = jnp.where(before, log_om, 0.0)
            hi = log_om.astype(BF16)
            lo = (log_om - hi.astype(F32)).astype(BF16)
            c = (jnp.dot(later_keys, hi, preferred_element_type=F32)
                 + jnp.dot(later_keys, lo, preferred_element_type=F32))
            cum.append(c)
            carry = carry_ref[hh:hh + 1, :]
            pre.append(log_sig + carry)
            carry_ref[hh:hh + 1, :] = carry + c[0:1, :] + log_om[0:1, :]
        for hh in range(2 * pg):
            a = jnp.exp(pre[hh] + cum[hh])
            if diagonal:
                a = jnp.where(before, a, 0.0)
            h = hh % 2
            acc_ref[hh] += jnp.dot(vt_ref[hh // 2, kb, h * HEAD_DIM:(h + 1) * HEAD_DIM, :], a.astype(BF16),
                                   preferred_element_type=F32)

    def some_weight_left():
        return (jnp.max(carry_ref[...]) >= EXP_UNDERFLOW).astype(jnp.int32)

    block_step(i, True)

    def kb_cond(state):
        j, go = state
        return jnp.logical_and(j < i, go == 1)

    def kb_body(state):
        j, _ = state
        block_step(i - 1 - j, False)
        return j + 1, some_weight_left()

    lax.while_loop(kb_cond, kb_body, (jnp.int32(0), some_weight_left()))
    for p in range(pg):
        o_ref[p] = jnp.concatenate([acc_ref[2 * p], acc_ref[2 * p + 1]], axis=0).T.astype(BF16)


def _sb_attention(proj, vt, batch, seq, t=T_BLOCK, pg=PAIRS_PER_STEP):
    nq = seq // t
    q_spec, k_spec, vt_spec, o_spec = _attn_specs(nq, seq, t, pg)
    return pl.pallas_call(
        functools.partial(_sb_kernel, t=t, pg=pg),
        grid=(batch, N_PAIRS // pg, nq),
        in_specs=[q_spec, k_spec, vt_spec],
        out_specs=o_spec,
        out_shape=jax.ShapeDtypeStruct((N_PAIRS, batch * seq, LANE), BF16),
        scratch_shapes=[pltpu.VMEM((2 * pg, HEAD_DIM, t), F32), pltpu.VMEM((2 * pg, t), F32),
                        pltpu.VMEM((2 * pg, t, LANE), BF16)],
        compiler_params=_params(3),
        name="stick_breaking_attention",
    )(proj, proj, vt)


def _softmax_block(hh, s, s_max, vt_rows, m_ref, acc_ref):
    m_prev = m_ref[hh]
    m_new = jnp.maximum(m_prev, s_max)
    m_safe = jnp.where(m_new == NEG_INF, 0.0, m_new)
    alpha = jnp.exp2(m_prev - m_safe)
    p = jnp.exp2(s - m_safe).astype(BF16)
    lhs = jnp.concatenate([vt_rows, jnp.ones((DENOM_ROWS, vt_rows.shape[1]), BF16)], axis=0)
    acc_ref[hh] = alpha * acc_ref[hh] + jnp.dot(lhs, p, preferred_element_type=F32)
    m_ref[hh] = m_new


def _reset_softmax_state(m_ref, acc_ref):
    m_ref[...] = jnp.full(m_ref.shape, NEG_INF, F32)
    acc_ref[...] = jnp.zeros_like(acc_ref)


def _normalized(acc_ref, hh, n_rows):
    acc = acc_ref[hh]
    return acc[:n_rows] / acc[n_rows:n_rows + 1]


def _score_tile_into(s_ref, smax_ref, qh_ref, k_ref, t, slot, kb, hh, bias):
    s = lax.dot_general(_key_block(k_ref, hh // 2, kb, t), qh_ref[hh], _NT, preferred_element_type=F32)
    if bias is not None:
        s = s + bias
    s_ref[slot, hh] = s
    smax_ref[slot, hh] = jnp.max(s, axis=0, keepdims=True)


def _pipelined_key_loop(i, n_heads, score_into, consume, bias_of=None):
    def step(next_slot, next_kb, cur_slot, cur_kb, diagonal):
        bias = None if (bias_of is None or next_slot is None) else bias_of(next_kb)
        for h in range(n_heads + SCORE_LEAD):
            if next_slot is not None and h < n_heads:
                score_into(next_slot, next_kb, h, bias)
            if cur_slot is not None and h >= SCORE_LEAD:
                consume(cur_slot, cur_kb, h - SCORE_LEAD, diagonal)

    step(0, 0, None, None, False)

    def pair_body(jj, _):
        kb = 2 * jj
        step(1, kb + 1, 0, kb, False)
        step(0, kb + 2, 1, kb + 1, False)
        return 0

    lax.fori_loop(0, i // 2, pair_body, 0)

    @pl.when(i % 2 == 1)
    def _():
        step(1, i, 0, i - 1, False)

    step(None, None, i % 2, i, True)


def _diff_kernel(lam_ref, subg_ref, q_ref, k_ref, vt_ref, o_ref, m_ref, acc_ref, s_ref, smax_ref, qh_ref, *, t, pg,
                 lambda_init):
    i = pl.program_id(2)
    _split_query_heads(q_ref, qh_ref, pg)
    key, qry = _key_query_iota(t)
    adm = (key // CHUNK) <= (qry // CHUNK)
    _reset_softmax_state(m_ref, acc_ref)

    def consume(slot, kb, hh, diagonal):
        s = s_ref[slot, hh]
        if diagonal:
            s = jnp.where(adm, s, NEG_INF)
            s_max = jnp.max(s, axis=0, keepdims=True)
        else:
            s_max = smax_ref[slot, hh]
        _softmax_block(hh, s, s_max, vt_ref[hh // 2, kb], m_ref, acc_ref)

    _pipelined_key_loop(i, 2 * pg, functools.partial(_score_tile_into, s_ref, smax_ref, qh_ref, k_ref, t), consume)

    lp = lam_ref[...]
    lam = (jnp.exp(jnp.sum(lp[0:1] * lp[1:2], axis=-1, keepdims=True))
           - jnp.exp(jnp.sum(lp[2:3] * lp[3:4], axis=-1, keepdims=True)) + lambda_init)
    for p in range(pg):
        d = _normalized(acc_ref, 2 * p, LANE) - lam * _normalized(acc_ref, 2 * p + 1, LANE)
        ms = jnp.mean(d * d, axis=0, keepdims=True)
        o_ref[p] = ((d * lax.rsqrt(ms + RMS_EPS)).T * subg_ref[...] * (1.0 - lambda_init)).astype(BF16)


def _diff_attention(proj, vt, lam_params, sub_g, lambda_init, batch, seq, t=T_BLOCK, pg=PAIRS_PER_STEP):
    nq = seq // t
    q_spec, k_spec, vt_spec, o_spec = _attn_specs(nq, seq, t, pg)
    return pl.pallas_call(
        functools.partial(_diff_kernel, t=t, pg=pg, lambda_init=lambda_init),
        grid=(batch, N_PAIRS // pg, nq),
        in_specs=[
            pl.BlockSpec((4, LANE), lambda b, h, i: (0, 0)),
            pl.BlockSpec((1, LANE), lambda b, h, i: (0, 0)),
            q_spec, k_spec, vt_spec,
        ],
        out_specs=o_spec,
        out_shape=jax.ShapeDtypeStruct((N_PAIRS, batch * seq, LANE), BF16),
        scratch_shapes=[pltpu.VMEM((2 * pg, 1, t), F32), pltpu.VMEM((2 * pg, LANE + DENOM_ROWS, t), F32),
                        pltpu.VMEM((2, 2 * pg, t, t), F32), pltpu.VMEM((2, 2 * pg, 1, t), F32),
                        pltpu.VMEM((2 * pg, t, LANE), BF16)],
        compiler_params=_params(3),
        name="differential_attention",
    )(lam_params, sub_g, proj, proj, vt)


def _dsa_attn_kernel(bias_ref, q_ref, k_ref, vt_ref, o_ref, m_ref, acc_ref, s_ref, smax_ref, qh_ref, *, t, pg):
    i = pl.program_id(2)
    _split_query_heads(q_ref, qh_ref, pg)
    _reset_softmax_state(m_ref, acc_ref)

    def consume(slot, kb, hh, diagonal):
        del diagonal
        h = hh % 2
        _softmax_block(hh, s_ref[slot, hh], smax_ref[slot, hh],
                       vt_ref[hh // 2, kb, h * HEAD_DIM:(h + 1) * HEAD_DIM, :], m_ref, acc_ref)

    def bias_of(kb):
        return bias_ref[0, 0, kb].astype(F32)

    _pipelined_key_loop(i, 2 * pg, functools.partial(_score_tile_into, s_ref, smax_ref, qh_ref, k_ref, t), consume,
                        bias_of)
    for p in range(pg):
        o_ref[p] = jnp.concatenate([_normalized(acc_ref, 2 * p, HEAD_DIM),
                                    _normalized(acc_ref, 2 * p + 1, HEAD_DIM)], axis=0).T.astype(BF16)


def _dsa_attention(proj, vt, bias, batch, seq, t=T_BLOCK, pg=PAIRS_PER_STEP):
    nq = seq // t
    q_spec, k_spec, vt_spec, o_spec = _attn_specs(nq, seq, t, pg)
    return pl.pallas_call(
        functools.partial(_dsa_attn_kernel, t=t, pg=pg),
        grid=(batch, N_PAIRS // pg, nq),
        in_specs=[
            pl.BlockSpec((1, 1, nq, t, t), lambda b, h, i: (b, i, 0, 0, 0)),
            q_spec, k_spec, vt_spec,
        ],
        out_specs=o_spec,
        out_shape=jax.ShapeDtypeStruct((N_PAIRS, batch * seq, LANE), BF16),
        scratch_shapes=[pltpu.VMEM((2 * pg, 1, t), F32), pltpu.VMEM((2 * pg, HEAD_DIM + DENOM_ROWS, t), F32),
                        pltpu.VMEM((2, 2 * pg, t, t), F32), pltpu.VMEM((2, 2 * pg, 1, t), F32),
                        pltpu.VMEM((2 * pg, t, LANE), BF16)],
        compiler_params=_params(3),
        name="dsa_sparse_attention",
    )(bias, proj, proj, vt)


def _dsa_select_kernel(qi_ref, ki_ref, wi_ref, bias_ref, key_ref, thr_ref, hi_ref, nge_ref, cnt_ref, run_ref, qh_ref,
                       wb_ref, lmax_ref, *, t, k_top):
    i = pl.program_id(1)
    nq = bias_ref.shape[2]
    n_groups = t // LANE
    r = lax.broadcasted_iota(jnp.int32, (t, t), 0)
    c = lax.broadcasted_iota(jnp.int32, (t, t), 1)
    adm = (c // CHUNK) <= (r // CHUNK)
    earlier_keys = jnp.where(r < c, 1.0, 0.0).astype(BF16)
    wi = wi_ref[0]
    _split_query_heads(qi_ref, qh_ref, IDX_PAIRS)
    for hh in range(IDX_HEADS):
        wb_ref[hh] = jnp.broadcast_to(wi[:, hh:hh + 1], (t, LANE))

    def head_scores(kb):
        ks = ki_ref[0, pl.ds(pl.multiple_of(kb * t, t), t), :]
        return [lax.dot_general(qh_ref[hh], ks, _NT, preferred_element_type=F32) for hh in range(IDX_HEADS)]

    def keys_from(rels, kb, diagonal):
        sc = None
        for hh in range(IDX_HEADS):
            term = jnp.concatenate([jnp.maximum(rels[hh][:, g * LANE:(g + 1) * LANE], 0.0) * wb_ref[hh]
                                    for g in range(n_groups)], axis=1)
            sc = term if sc is None else sc + term
        if diagonal:
            sc = jnp.where(adm, sc, NEG_INF)
        bits = pltpu.bitcast(sc, jnp.int32)
        srt = bits ^ ((bits >> 31) & 0x7FFFFFFF)
        srt = jnp.where(sc == 0.0, 0, srt)
        key_ref[kb] = srt
        for g in range(n_groups):
            lmax_ref[...] = jnp.maximum(lmax_ref[...], srt[:, g * LANE:(g + 1) * LANE])

    def score_pair_body(jj, _):
        first, second = head_scores(2 * jj), head_scores(2 * jj + 1)
        keys_from(first, 2 * jj, False)
        keys_from(second, 2 * jj + 1, False)
        return 0

    lmax_ref[...] = jnp.full(lmax_ref.shape, INT32_MIN, jnp.int32)
    lax.fori_loop(0, i // 2, score_pair_body, 0)

    @pl.when(i % 2 == 1)
    def _():
        keys_from(head_scores(i - 1), i - 1, False)

    keys_from(head_scores(i), i, True)

    def count_where(pred):
        cnt_ref[...] = jnp.zeros_like(cnt_ref)

        def body(kb, _):
            kk = key_ref[kb]
            part = cnt_ref[...]
            for g in range(n_groups):
                part = part + jnp.where(pred(kk[:, g * LANE:(g + 1) * LANE]), 1.0, 0.0)
            cnt_ref[...] = part
            return 0

        lax.fori_loop(0, i + 1, body, 0)
        return jnp.sum(cnt_ref[...], axis=-1, keepdims=True)

    row_max = jnp.broadcast_to(jnp.max(lmax_ref[...], axis=-1, keepdims=True), thr_ref.shape)
    guess = jnp.where(row_max >= INT32_MIN + BRACKET, row_max - BRACKET, INT32_MIN)
    n_guess = count_where(lambda kk: kk >= guess)
    guess_ok = n_guess >= float(k_top)
    thr_ref[...] = jnp.where(guess_ok, guess, INT32_MIN)
    nge_ref[...] = jnp.where(guess_ok, n_guess, float(t) * (i + 1).astype(F32)) + jnp.zeros(nge_ref.shape, F32)
    hi_ref[...] = row_max + 1

    def bis_body(_, carry):
        lo, hi = thr_ref[...], hi_ref[...]
        mid = lo + lax.shift_right_logical(hi - lo, 1)
        n_ge = count_where(lambda kk: kk >= mid)
        up = n_ge >= float(k_top)
        thr_ref[...] = jnp.where(up, mid, lo)
        hi_ref[...] = jnp.where(up, hi, mid)
        nge_ref[...] = jnp.where(up, n_ge, nge_ref[...])
        return carry

    all_guessed = jnp.min(jnp.where(guess_ok, 1, 0)) == 1
    lax.fori_loop(0, jnp.where(all_guessed, BRACKET_STEPS, 32), bis_body, 0)
    thr = thr_ref[...]

    surplus_ties = jnp.max(nge_ref[...]) > float(k_top)

    def plain_block(kb, diagonal):
        kk = key_ref[kb]
        bias = jnp.concatenate(
            [jnp.where(kk[:, g * LANE:(g + 1) * LANE] >= thr, 0.0, NEG_INF) for g in range(n_groups)], axis=1)
        if diagonal:
            bias = jnp.where(adm, bias, NEG_INF)
        bias_ref[0, 0, kb] = bias.T.astype(BF16)

    @pl.when(jnp.logical_not(surplus_ties))
    def _():
        def plain_body(kb, _):
            plain_block(kb, False)
            return 0

        lax.fori_loop(0, i, plain_body, 0)
        plain_block(i, True)

    def tie_block(kb, diagonal, need):
        kk = key_ref[kb]
        eq = jnp.concatenate(
            [jnp.where(kk[:, g * LANE:(g + 1) * LANE] == thr, 1.0, 0.0) for g in range(n_groups)], axis=1)
        rank = jnp.dot(eq.astype(BF16), earlier_keys, preferred_element_type=F32)
        parts = []
        for g in range(n_groups):
            sl = slice(g * LANE, (g + 1) * LANE)
            tie_ok = (rank[:, sl] + run_ref[...]) < need
            parts.append(jnp.where(kk[:, sl] > thr, 0.0,
                                   jnp.where(kk[:, sl] == thr, jnp.where(tie_ok, 0.0, NEG_INF), NEG_INF)))
        bias = jnp.concatenate(parts, axis=1)
        if diagonal:
            bias = jnp.where(adm, bias, NEG_INF)
        bias_ref[0, 0, kb] = bias.T.astype(BF16)
        run_ref[...] = run_ref[...] + jnp.sum(eq, axis=-1, keepdims=True)

    @pl.when(surplus_ties)
    def _():
        need = float(k_top) - count_where(lambda kk: kk > thr)
        run_ref[...] = jnp.zeros_like(run_ref)

        def tie_body(kb, _):
            tie_block(kb, False, need)
            return 0

        lax.fori_loop(0, i, tie_body, 0)
        tie_block(i, True, need)

    def fill_body(kb, _):
        bias_ref[0, 0, kb] = jnp.full((t, t), NEG_INF, BF16)
        return 0

    lax.fori_loop(i + 1, nq, fill_body, 0)


def _dsa_select(proj, wi, batch, seq, t=T_BLOCK):
    nq = seq // t
    k_top = min(TOPK_MAX, seq // 4)
    return pl.pallas_call(
        functools.partial(_dsa_select_kernel, t=t, k_top=k_top),
        grid=(batch, nq),
        in_specs=[
            pl.BlockSpec((IDX_PAIRS, t, LANE), lambda b, i: (QI_BLOCK0 // IDX_PAIRS, b * nq + i, 0)),
            pl.BlockSpec((1, seq, LANE), lambda b, i: (KI_BLOCK, b, 0)),
            pl.BlockSpec((1, t, LANE), lambda b, i: (0, b * nq + i, 0)),
        ],
        out_specs=pl.BlockSpec((1, 1, nq, t, t), lambda b, i: (b, i, 0, 0, 0)),
        out_shape=jax.ShapeDtypeStruct((batch, nq, nq, t, t), BF16),
        scratch_shapes=[
            pltpu.VMEM((nq, t, t), jnp.int32),
            pltpu.VMEM((t, LANE), jnp.int32),
            pltpu.VMEM((t, LANE), jnp.int32),
            pltpu.VMEM((t, LANE), F32),
            pltpu.VMEM((t, LANE), F32),
            pltpu.VMEM((t, LANE), F32),
            pltpu.VMEM((IDX_HEADS, t, LANE), BF16),
            pltpu.VMEM((IDX_HEADS, t, LANE), F32),
            pltpu.VMEM((t, LANE), jnp.int32),
        ],
        compiler_params=_params(2),
        name="dsa_indexer_select",
    )(proj, proj, wi)


def _rope_tables(seq):
    inv = 1.0 / (ROPE_THETA ** (jnp.arange(0, HEAD_DIM, 2, dtype=F32) / HEAD_DIM))
    ang = jnp.arange(seq, dtype=F32)[:, None] * inv[None, :]
    cos, sin = jnp.cos(ang), jnp.sin(ang)
    zero = jnp.zeros_like(sin)
    n_heads = LANE // HEAD_DIM
    cos_t = jnp.tile(cos, (1, 2 * n_heads))
    sin_first = jnp.tile(jnp.concatenate([-sin, zero], axis=1), (1, n_heads))
    sin_second = jnp.tile(jnp.concatenate([zero, sin], axis=1), (1, n_heads))
    return cos_t, sin_first, sin_second


def _stream_ops(q_rope, k_rope, q_scale):
    q_ops = tuple((q_rope, q_scale, ("main", Q_BLOCK0 + p)) for p in range(N_PAIRS))
    k_ops = tuple((k_rope, 1.0, ("main", K_BLOCK0 + p)) for p in range(N_PAIRS))
    v_ops = tuple((False, 1.0, ("vt", p)) for p in range(N_PAIRS))
    g_ops = tuple((False, 1.0, ("main", G_BLOCK0 + p)) for p in range(N_PAIRS))
    return q_ops + k_ops + v_ops + g_ops


def _one_layer(layer, x2d, params, rope_tabs, batch, seq):
    d = x2d.shape[1]
    mixer, j = layer % N_MIXERS, layer // N_MIXERS
    main = 4 * INNER
    if mixer == 0:
        w = params["w_in_a"][j]
        w_idx_q = w[:, main:main + IDX_HEADS * IDX_DIM]
        w_idx_w = w[:, main + IDX_HEADS * IDX_DIM:main + IDX_HEADS * IDX_DIM + IDX_HEADS]
        w_idx_k = w[:, main + IDX_HEADS * IDX_DIM + IDX_HEADS:]
        w_pad = jnp.concatenate(
            [w[:, :main], w_idx_q, w_idx_k, w_idx_k, w_idx_w, jnp.zeros((d, LANE - IDX_HEADS), w.dtype)],
            axis=1).astype(BF16)
        ops = (_stream_ops(True, True, LOGIT_SCALE * LOG2E)
               + tuple((True, IDX_DIM ** -0.5, ("main", QI_BLOCK0 + p)) for p in range(IDX_PAIRS))
               + ((True, 1.0, ("main", KI_BLOCK)), (False, IDX_HEADS ** -0.5, ("f32", 0))))
        proj, vt, wi = _in_proj(x2d, w_pad, rope_tabs, ops, seq)
        bias = _dsa_select(proj, wi, batch, seq)
        o = _dsa_attention(proj, vt, bias, batch, seq)
        w_out = params["w_out_a"][j]
    elif mixer == 1:
        proj, vt, _ = _in_proj(x2d, params["w_in_b"][j].astype(BF16), rope_tabs, _stream_ops(False, False, LOGIT_SCALE), seq)
        o = _sb_attention(proj, vt, batch, seq)
        w_out = params["w_out_b"][j]
    else:
        lambda_init = 0.8 - 0.6 * math.exp(-0.3 * layer)
        proj, vt, _ = _in_proj(x2d, params["w_in_c"][j].astype(BF16), rope_tabs, _stream_ops(True, True, LOGIT_SCALE * LOG2E),
                               seq)
        lam_rows = jnp.stack([params["lambda_q1"][j], params["lambda_k1"][j],
                              params["lambda_q2"][j], params["lambda_k2"][j]]).astype(F32)
        lam_params = jnp.pad(lam_rows, ((0, 0), (0, LANE - DIFF_DIM)))
        sub_g = params["subln_g"][j].reshape(1, 2 * DIFF_DIM).astype(F32)
        o = _diff_attention(proj, vt, lam_params, sub_g, lambda_init, batch, seq)
        w_out = params["w_out_c"][j]
    return _out_proj_ln(o, proj, x2d, w_out.astype(BF16), params["ln_g"][layer], params["ln_b"][layer])


def kernel(x, w_in_a, w_out_a, w_in_b, w_out_b, w_in_c, w_out_c,
           lambda_q1, lambda_k1, lambda_q2, lambda_k2, subln_g, ln_g, ln_b):
    batch, seq, d = x.shape
    params = dict(w_in_a=w_in_a, w_out_a=w_out_a, w_in_b=w_in_b, w_out_b=w_out_b, w_in_c=w_in_c, w_out_c=w_out_c,
                  lambda_q1=lambda_q1, lambda_k1=lambda_k1, lambda_q2=lambda_q2, lambda_k2=lambda_k2,
                  subln_g=subln_g, ln_g=ln_g, ln_b=ln_b)
    x2d = x.reshape(batch * seq, d)
    rope_tabs = _rope_tables(seq)
    for layer in range(DEPTH):
        x2d = _one_layer(layer, x2d, params, rope_tabs, batch, seq)
    return x2d.reshape(batch, seq, d)
```
